```python
import math
import jax, jax.numpy as jnp
from jax import lax
import numpy as np

D_MODEL = 1024
BATCH = 16
SEQ = 2048
DEPTH = 1

HEAD_DIM = 64
DIFF_WIDTH = D_MODEL // 2
N_DIFF_HEADS = DIFF_WIDTH // (2 * HEAD_DIM)
SB_WIDTH = D_MODEL - DIFF_WIDTH
N_SB_HEADS = SB_WIDTH // HEAD_DIM
MIX_WIDTH = DIFF_WIDTH + SB_WIDTH
D_FF = 4 * D_MODEL
Q_BLOCK = 128
EPS = 1e-6
LAMBDA_STD = 0.1

kernel_name = "hymba_diffattn_stickbreaking_sqrelu"


def rmsnorm(x, gain):
    xf = x.astype(jnp.float32)
    inv = lax.rsqrt(jnp.mean(xf * xf, axis=-1, keepdims=True) + EPS)
    return (xf * inv * gain.astype(jnp.float32)).astype(x.dtype)


def lambda_init_fn(layer_idx):
    return 0.8 - 0.6 * math.exp(-0.3 * layer_idx)


def alibi_slopes(n_heads):
    return jnp.asarray([2.0 ** (-8.0 * (h + 1) / n_heads) for h in range(n_heads)], dtype=jnp.float32)


def diff_attn_block(q1, q2, k1, k2, v, lam, slopes, q0):
    tq, tk = q1.shape[2], k1.shape[2]
    qpos = q0 + jnp.arange(tq)
    kpos = jnp.arange(tk)
    dist = (qpos[:, None] - kpos[None, :]).astype(jnp.float32)
    causal = dist >= 0
    bias = -slopes[:, None, None] * dist
    scale = 1.0 / math.sqrt(HEAD_DIM)

    def probs(q, k):
        s = jnp.einsum('bhqd,bhkd->bhqk', q, k).astype(jnp.float32) * scale + bias
        s = jnp.where(causal, s, -jnp.inf)
        return jax.nn.softmax(s, axis=-1)

    a = probs(q1, k1) - lam * probs(q2, k2)
    return jnp.einsum('bhqk,bhkd->bhqd', a.astype(v.dtype), v)


def stick_breaking_block(q, k, v, q0):
    tq, tk = q.shape[2], k.shape[2]
    qpos = q0 + jnp.arange(tq)
    kpos = jnp.arange(tk)
    strict = kpos[None, :] < qpos[:, None]
    z = jnp.einsum('bhqd,bhkd->bhqk', q, k).astype(jnp.float32) * (1.0 / math.sqrt(HEAD_DIM))
    log_beta = jax.nn.log_sigmoid(z)
    log_om = jnp.where(strict, jax.nn.log_sigmoid(-z), 0.0)
    later = lax.cumsum(log_om, axis=3, reverse=True) - log_om
    w = jnp.where(strict, jnp.exp(log_beta + later), 0.0)
    return jnp.einsum('bhqk,bhkd->bhqd', w.astype(v.dtype), v)


def to_heads(t, n_heads, dim):
    b, s, _ = t.shape
    return t.reshape(b, s, n_heads, dim).transpose(0, 2, 1, 3)


def hybrid_mixer(h, w_in, lq1, lk1, lq2, lk2, diff_g, sb_g, w_out, layer_idx):
    b, s, _ = h.shape
    proj = jnp.einsum('bsd,de->bse', h, w_in)
    dq, dk, dv, sq, sk, sv = jnp.split(
        proj, np.cumsum([DIFF_WIDTH, DIFF_WIDTH, DIFF_WIDTH, SB_WIDTH, SB_WIDTH])[:].tolist(), axis=-1)
    dq = to_heads(dq, 2 * N_DIFF_HEADS, HEAD_DIM)
    dk = to_heads(dk, 2 * N_DIFF_HEADS, HEAD_DIM)
    q1, q2 = dq[:, 0::2], dq[:, 1::2]
    k1, k2 = dk[:, 0::2], dk[:, 1::2]
    dv = to_heads(dv, N_DIFF_HEADS, 2 * HEAD_DIM)
    sq = to_heads(sq, N_SB_HEADS, HEAD_DIM)
    sk = to_heads(sk, N_SB_HEADS, HEAD_DIM)
    sv = to_heads(sv, N_SB_HEADS, HEAD_DIM)

    lam_init = lambda_init_fn(layer_idx)
    lam = (jnp.exp(jnp.sum(lq1.astype(jnp.float32) * lk1.astype(jnp.float32)))
           - jnp.exp(jnp.sum(lq2.astype(jnp.float32) * lk2.astype(jnp.float32)))
           + lam_init)
    slopes = alibi_slopes(N_DIFF_HEADS)

    diff_out, sb_out = [], []
    for i in range(s // Q_BLOCK):
        q0 = i * Q_BLOCK
        qe = q0 + Q_BLOCK
        diff_out.append(diff_attn_block(q1[:, :, q0:qe], q2[:, :, q0:qe], k1[:, :, :qe], k2[:, :, :qe],
                                        dv[:, :, :qe], lam, slopes, q0))
        sb_out.append(stick_breaking_block(sq[:, :, q0:qe], sk[:, :, :qe], sv[:, :, :qe], q0))
    diff_o = jnp.concatenate(diff_out, axis=2)
    sb_o = jnp.concatenate(sb_out, axis=2)

    diff_o = rmsnorm(diff_o, diff_g) * (1.0 - lam_init)
    sb_o = rmsnorm(sb_o, sb_g)
    diff_o = diff_o.transpose(0, 2, 1, 3).reshape(b, s, DIFF_WIDTH)
    sb_o = sb_o.transpose(0, 2, 1, 3).reshape(b, s, SB_WIDTH)
    mixed = jnp.concatenate([diff_o, sb_o], axis=-1)
    return jnp.einsum('bse,ed->bsd', mixed, w_out)


def setup_inputs(seed: int = 0) -> dict:
    key = jax.random.key(seed)
    ks = jax.random.split(key, 14)
    f32 = jnp.float32
    n = jax.random.normal
    return {
        "x": n(ks[0], (BATCH, SEQ, D_MODEL), f32),
        "attn_norm": 1.0 + 0.02 * n(ks[1], (DEPTH, D_MODEL), f32),
        "w_in": n(ks[2], (DEPTH, D_MODEL, 3 * MIX_WIDTH), f32) * D_MODEL ** -0.5,
        "lambda_q1": LAMBDA_STD * n(ks[3], (DEPTH, HEAD_DIM), f32),
        "lambda_k1": LAMBDA_STD * n(ks[4], (DEPTH, HEAD_DIM), f32),
        "lambda_q2": LAMBDA_STD * n(ks[5], (DEPTH, HEAD_DIM), f32),
        "lambda_k2": LAMBDA_STD * n(ks[6], (DEPTH, HEAD_DIM), f32),
        "diff_subln": 1.0 + 0.02 * n(ks[7], (DEPTH, 2 * HEAD_DIM), f32),
        "sb_subln": 1.0 + 0.02 * n(ks[8], (DEPTH, HEAD_DIM), f32),
        "w_out": n(ks[9], (DEPTH, MIX_WIDTH, D_MODEL), f32) * MIX_WIDTH ** -0.5,
        "mlp_norm": 1.0 + 0.02 * n(ks[10], (DEPTH, D_MODEL), f32),
        "w_up": n(ks[11], (DEPTH, D_MODEL, D_FF), f32) * D_MODEL ** -0.5,
        "w_down": n(ks[12], (DEPTH, D_FF, D_MODEL), f32) * D_FF ** -0.5,
        "final_norm": 1.0 + 0.02 * n(ks[13], (D_MODEL,), f32),
    }


def reference(x, attn_norm, w_in, lambda_q1, lambda_k1, lambda_q2, lambda_k2, diff_subln, sb_subln,
              w_out, mlp_norm, w_up, w_down, final_norm):
    h = x
    for l in range(DEPTH):
        a = rmsnorm(h, attn_norm[l])
        h = h + hybrid_mixer(a, w_in[l], lambda_q1[l], lambda_k1[l], lambda_q2[l], lambda_k2[l],
                             diff_subln[l], sb_subln[l], w_out[l], l)
        m = rmsnorm(h, mlp_norm[l])
        u = jnp.square(jax.nn.relu(jnp.einsum('bsd,df->bsf', m, w_up[l])))
        h = h + jnp.einsum('bsf,fd->bsd', u, w_down[l])
    return rmsnorm(h, final_norm)
```

```python
import functools
import math

import jax
import jax.numpy as jnp
from jax import lax
from jax.experimental import pallas as pl
from jax.experimental.pallas import tpu as pltpu

D_MODEL = 1024
HEAD_DIM = 64
DIFF_WIDTH = D_MODEL // 2
N_DIFF_HEADS = DIFF_WIDTH // (2 * HEAD_DIM)
SB_WIDTH = D_MODEL - DIFF_WIDTH
N_SB_HEADS = SB_WIDTH // HEAD_DIM
N_SB_PAIRS = N_SB_HEADS // 2
MIX_WIDTH = DIFF_WIDTH + SB_WIDTH
D_FF = 4 * D_MODEL
EPS = 1e-6
LAYER_IDX = 0
LAM_INIT = 0.8 - 0.6 * math.exp(-0.3 * LAYER_IDX)
ATTN_SCALE = 1.0 / math.sqrt(HEAD_DIM)

LANES = 128
ATTN_TILE = 256
TOKEN_TILE = 512
FF_CHUNK = 1024
MASKED = -1e30
VMEM_LIMIT_BYTES = 56 * 1024 * 1024

_F32 = jnp.float32
_BF16 = jnp.bfloat16


def _rmsnorm(x, gain):
    inv = lax.rsqrt(jnp.mean(x * x, axis=-1, keepdims=True) + EPS)
    return x * inv * gain


def _dot(a, b):
    return jnp.dot(a, b, preferred_element_type=_F32)


def _dot_nt(a, b):
    return lax.dot_general(a, b, (((1,), (1,)), ((), ())), preferred_element_type=_F32)


def _in_proj_kernel(x_ref, g_ref, w_ref, o_ref):
    a = _rmsnorm(x_ref[...], g_ref[...])
    o_ref[...] = _dot(a.astype(_BF16), w_ref[...]).astype(_BF16)


def _in_proj(x2d, gain, w_bf16):
    n_tok = x2d.shape[0]
    n_out = w_bf16.shape[1]
    return pl.pallas_call(
        _in_proj_kernel,
        grid=(n_tok // TOKEN_TILE,),
        in_specs=[
            pl.BlockSpec((TOKEN_TILE, D_MODEL), lambda i: (i, 0)),
            pl.BlockSpec((1, D_MODEL), lambda i: (0, 0)),
            pl.BlockSpec((D_MODEL, n_out), lambda i: (0, 0)),
        ],
        out_specs=pl.BlockSpec((TOKEN_TILE, n_out), lambda i: (i, 0)),
        out_shape=jax.ShapeDtypeStruct((n_tok, n_out), _BF16),
        compiler_params=pltpu.CompilerParams(vmem_limit_bytes=VMEM_LIMIT_BYTES),
        name="in_proj",
    )(x2d, gain, w_bf16)


def _half_masks():
    lane = lax.broadcasted_iota(jnp.int32, (1, LANES), 1)
    return lane < HEAD_DIM, lane >= HEAD_DIM


def _split_halves(x_bf16, scale):
    lo, hi = _half_masks()
    xf = x_bf16.astype(_F32) * scale
    return (jnp.where(lo, xf, 0.0).astype(_BF16), jnp.where(hi, xf, 0.0).astype(_BF16))


def _tile_rows(ref, blk):
    return ref[pl.ds(pl.multiple_of(blk * ATTN_TILE, ATTN_TILE), ATTN_TILE), :]


def _diff_attn_kernel(slopes_ref, lq1_ref, lk1_ref, lq2_ref, lk2_ref, g_ref,
                      q_ref, k_ref, v_ref, o_ref,
                      m1_ref, l1_ref, a1_ref, m2_ref, l2_ref, a2_ref, dbias_ref):
    t = ATTN_TILE
    reps = t // LANES
    n_blk = q_ref.shape[0] // t
    slope = slopes_ref[pl.program_id(1)]

    lam = (jnp.exp(jnp.sum(lq1_ref[...] * lk1_ref[...], axis=-1, keepdims=True))
           - jnp.exp(jnp.sum(lq2_ref[...] * lk2_ref[...], axis=-1, keepdims=True))
           + LAM_INIT)

    row = lax.broadcasted_iota(jnp.int32, (t, t), 0)
    col = lax.broadcasted_iota(jnp.int32, (t, t), 1)
    dbias_ref[...] = jnp.where(row >= col, slope * col.astype(_F32), MASKED)
    colbias = slope * lax.broadcasted_iota(jnp.int32, (1, t), 1).astype(_F32)

    stats = ((m1_ref, l1_ref, a1_ref), (m2_ref, l2_ref, a2_ref))

    def tile_step(q_halves, k, v, bias, off, first):
        for qh, (m_ref, l_ref, a_ref) in zip(q_halves, stats):
            s = _dot_nt(qh, k) + bias
            m_curr = jnp.max(s, axis=1, keepdims=True) + off
            if first:
                m_new = jnp.broadcast_to(m_curr, (t, LANES))
            else:
                m_prev = m_ref[...]
                m_new = jnp.maximum(m_prev, m_curr)
            p = jnp.exp(s - pltpu.repeat(m_new - off, reps, 1))
            p_sum = jnp.sum(p, axis=1, keepdims=True)
            pv = _dot(p.astype(_BF16), v)
            if first:
                l_ref[...] = jnp.broadcast_to(p_sum, (t, LANES))
                a_ref[...] = pv
            else:
                alpha = jnp.exp(m_prev - m_new)
                l_ref[...] = alpha * l_ref[...] + p_sum
                a_ref[...] = alpha * a_ref[...] + pv
            m_ref[...] = m_new

    def q_block(qi, carry):
        q_halves = _split_halves(_tile_rows(q_ref, qi), ATTN_SCALE)
        base = (slope * t) * qi.astype(_F32)
        tile_step(q_halves, _tile_rows(k_ref, qi), _tile_rows(v_ref, qi),
                  dbias_ref[...], base, True)

        def k_block(kj, c):
            off = (slope * t) * kj.astype(_F32)
            tile_step(q_halves, _tile_rows(k_ref, kj), _tile_rows(v_ref, kj),
                      colbias, off, False)
            return c

        lax.fori_loop(0, qi, k_block, 0)

        o = a1_ref[...] / l1_ref[...] - lam * (a2_ref[...] / l2_ref[...])
        o = _rmsnorm(o, g_ref[...]) * (1.0 - LAM_INIT)
        o_ref[pl.ds(pl.multiple_of(qi * t, t), t), :] = o.astype(o_ref.dtype)
        return carry

    lax.fori_loop(0, n_blk, q_block, 0)


def _diff_attn(qkv, slopes, lq1, lk1, lq2, lk2, gain):
    b, s, _ = qkv.shape
    n_h = N_DIFF_HEADS
    k_off = DIFF_WIDTH // LANES
    v_off = 2 * DIFF_WIDTH // LANES
    vec = lambda n: pl.BlockSpec((1, n), lambda bi, hi: (0, 0))
    blk = lambda off: pl.BlockSpec((None, s, LANES), lambda bi, hi: (bi, 0, off + hi))
    stat = pltpu.VMEM((ATTN_TILE, LANES), _F32)
    return pl.pallas_call(
        _diff_attn_kernel,
        grid=(b, n_h),
        in_specs=[
            pl.BlockSpec(memory_space=pltpu.SMEM),
            vec(HEAD_DIM), vec(HEAD_DIM), vec(HEAD_DIM), vec(HEAD_DIM), vec(LANES),
            blk(0), blk(k_off), blk(v_off),
        ],
        out_specs=pl.BlockSpec((None, s, LANES), lambda bi, hi: (bi, 0, hi)),
        out_shape=jax.ShapeDtypeStruct((b, s, DIFF_WIDTH), _BF16),
        scratch_shapes=[stat, stat, stat, stat, stat, stat,
                        pltpu.VMEM((ATTN_TILE, ATTN_TILE), _F32)],
        compiler_params=pltpu.CompilerParams(vmem_limit_bytes=VMEM_LIMIT_BYTES),
        name="diff_attn",
    )(slopes, lq1, lk1, lq2, lk2, gain, qkv, qkv, qkv)


def _sb_attn_kernel(g_ref, q_ref, k_ref, v_ref, o_ref,
                    ca_ref, cb_ref, acc_ref, tri_ref, dmask_ref):
    t = ATTN_TILE
    reps = t // LANES
    n_blk = q_ref.shape[0] // t

    row = lax.broadcasted_iota(jnp.int32, (t, t), 0)
    col = lax.broadcasted_iota(jnp.int32, (t, t), 1)
    tri_ref[...] = jnp.where(row > col, 1.0, 0.0).astype(_BF16)
    dmask_ref[...] = jnp.where(col < row, 0.0, MASKED)
    lo, hi = _half_masks()

    def tile_step(q_halves, k, v_halves, mask, first):
        pv = None
        for qh, vh, c_ref in zip(q_halves, v_halves, (ca_ref, cb_ref)):
            z = _dot_nt(qh, k)
            if mask is not None:
                z = z + mask
            sp = jnp.log(1.0 + jnp.exp(-jnp.abs(z)))
            log_beta = jnp.minimum(z, 0.0) - sp
            log_om = log_beta - z
            later = _dot(log_om.astype(_BF16), tri_ref[...])
            om_sum = jnp.sum(log_om, axis=1, keepdims=True)
            if first:
                w = jnp.exp(log_beta + later)
                c_ref[...] = jnp.broadcast_to(om_sum, (t, LANES))
            else:
                carry = c_ref[...]
                w = jnp.exp(log_beta + later + pltpu.repeat(carry, reps, 1))
                c_ref[...] = carry + om_sum
            d = _dot(w.astype(_BF16), vh)
            pv = d if pv is None else pv + d
        if first:
            acc_ref[...] = pv
        else:
            acc_ref[...] += pv

    def v_halves_of(v):
        return _split_halves(v, 1.0)

    def q_block(qi, carry):
        q_halves = _split_halves(_tile_rows(q_ref, qi), ATTN_SCALE)
        tile_step(q_halves, _tile_rows(k_ref, qi), v_halves_of(_tile_rows(v_ref, qi)),
                  dmask_ref[...], True)

        def k_block(i, c):
            kj = qi - 1 - i
            tile_step(q_halves, _tile_rows(k_ref, kj), v_halves_of(_tile_rows(v_ref, kj)),
                      None, False)
            return c

        lax.fori_loop(0, qi, k_block, 0)

        o = acc_ref[...]
        sq = o * o
        ms_lo = jnp.sum(jnp.where(lo, sq, 0.0), axis=-1, keepdims=True) * (1.0 / HEAD_DIM)
        ms_hi = jnp.sum(jnp.where(hi, sq, 0.0), axis=-1, keepdims=True) * (1.0 / HEAD_DIM)
        inv = jnp.where(lo, lax.rsqrt(ms_lo + EPS), lax.rsqrt(ms_hi + EPS))
        o = o * inv * g_ref[...]
        o_ref[pl.ds(pl.multiple_of(qi * t, t), t), :] = o.astype(o_ref.dtype)
        return carry

    lax.fori_loop(0, n_blk, q_block, 0)


def _sb_attn(qkv, gain2):
    b, s, _ = qkv.shape
    q_off = 3 * DIFF_WIDTH // LANES
    k_off = q_off + SB_WIDTH // LANES
    v_off = k_off + SB_WIDTH // LANES
    blk = lambda off: pl.BlockSpec((None, s, LANES), lambda bi, pi: (bi, 0, off + pi))
    stat = pltpu.VMEM((ATTN_TILE, LANES), _F32)
    return pl.pallas_call(
        _sb_attn_kernel,
        grid=(b, N_SB_PAIRS),
        in_specs=[
            pl.BlockSpec((1, LANES), lambda bi, pi: (0, 0)),
            blk(q_off), blk(k_off), blk(v_off),
        ],
        out_specs=pl.BlockSpec((None, s, LANES), lambda bi, pi: (bi, 0, pi)),
        out_shape=jax.ShapeDtypeStruct((b, s, SB_WIDTH), _BF16),
        scratch_shapes=[stat, stat, stat,
                        pltpu.VMEM((ATTN_TILE, ATTN_TILE), _BF16),
                        pltpu.VMEM((ATTN_TILE, ATTN_TILE), _F32)],
        compiler_params=pltpu.CompilerParams(vmem_limit_bytes=VMEM_LIMIT_BYTES),
        name="sb_attn",
    )(gain2, qkv, qkv, qkv)


def _out_mlp_kernel(x_ref, md_ref, ms_ref, wod_ref, wos_ref, g2_ref, wup_ref, wdn_ref,
                    g3_ref, o_ref, m_ref, acc_ref):
    h = x_ref[...] + _dot(md_ref[...], wod_ref[...]) + _dot(ms_ref[...], wos_ref[...])
    m_ref[...] = _rmsnorm(h, g2_ref[...]).astype(_BF16)
    acc_ref[...] = h

    def ff_chunk(c, carry):
        cols = pl.ds(pl.multiple_of(c * FF_CHUNK, FF_CHUNK), FF_CHUNK)
        u = jnp.square(jnp.maximum(_dot(m_ref[...], wup_ref[:, cols]), 0.0))
        acc_ref[...] += _dot(u.astype(_BF16), wdn_ref[cols, :])
        return carry

    lax.fori_loop(0, D_FF // FF_CHUNK, ff_chunk, 0)
    o_ref[...] = _rmsnorm(acc_ref[...], g3_ref[...])


def _out_mlp(x2d, mix_d, mix_s, wo_d, wo_s, g2, w_up, w_dn, g3):
    n_tok = x2d.shape[0]
    const = lambda shape: pl.BlockSpec(shape, lambda i: (0, 0), pipeline_mode=pl.Buffered(1))
    return pl.pallas_call(
        _out_mlp_kernel,
        grid=(n_tok // TOKEN_TILE,),
        in_specs=[
            pl.BlockSpec((TOKEN_TILE, D_MODEL), lambda i: (i, 0)),
            pl.BlockSpec((TOKEN_TILE, DIFF_WIDTH), lambda i: (i, 0)),
            pl.BlockSpec((TOKEN_TILE, SB_WIDTH), lambda i: (i, 0)),
            const((DIFF_WIDTH, D_MODEL)),
            const((SB_WIDTH, D_MODEL)),
            const((1, D_MODEL)),
            const((D_MODEL, D_FF)),
            const((D_FF, D_MODEL)),
            const((1, D_MODEL)),
        ],
        out_specs=pl.BlockSpec((TOKEN_TILE, D_MODEL), lambda i: (i, 0)),
        out_shape=jax.ShapeDtypeStruct((n_tok, D_MODEL), _F32),
        scratch_shapes=[pltpu.VMEM((TOKEN_TILE, D_MODEL), _BF16),
                        pltpu.VMEM((TOKEN_TILE, D_MODEL), _F32)],
        compiler_params=pltpu.CompilerParams(vmem_limit_bytes=VMEM_LIMIT_BYTES),
        name="out_mlp",
    )(x2d, mix_d, mix_s, wo_d, wo_s, g2, w_up, w_dn, g3)


def kernel(x, attn_norm, w_in, lambda_q1, lambda_k1, lambda_q2, lambda_k2, diff_subln, sb_subln,
           w_out, mlp_norm, w_up, w_down, final_norm):
    assert attn_norm.shape[0] == 1, "single-layer block"
    b, s, d = x.shape
    x2d = x.reshape(b * s, d)

    qkv = _in_proj(x2d, attn_norm, w_in[0].astype(_BF16)).reshape(b, s, 3 * MIX_WIDTH)

    slopes = jnp.asarray([2.0 ** (-8.0 * (h + 1) / N_DIFF_HEADS) for h in range(N_DIFF_HEADS)],
                         dtype=_F32)
    mix_d = _diff_attn(qkv, slopes, lambda_q1, lambda_k1, lambda_q2, lambda_k2, diff_subln)
    mix_s = _sb_attn(qkv, jnp.tile(sb_subln, (1, 2)))

    w_out_bf16 = w_out[0].astype(_BF16)
    out = _out_mlp(x2d, mix_d.reshape(b * s, DIFF_WIDTH), mix_s.reshape(b * s, SB_WIDTH),
                   w_out_bf16[:DIFF_WIDTH], w_out_bf16[DIFF_WIDTH:], mlp_norm,
                   w_up[0].astype(_BF16), w_down[0].astype(_BF16), final_norm.reshape(1, d))
    return out.reshape(b, s, d)
```

```python
import math

import jax
import jax.numpy as jnp
from jax import lax
from jax.experimental import pallas as pl
from jax.experimental.pallas import tpu as pltpu

D_MODEL = 1024
HEAD_DIM = 64
DIFF_WIDTH = D_MODEL // 2
N_DIFF_HEADS = DIFF_WIDTH // (2 * HEAD_DIM)
SB_WIDTH = D_MODEL - DIFF_WIDTH
N_SB_HEADS = SB_WIDTH // HEAD_DIM
N_SB_PAIRS = N_SB_HEADS // 2
MIX_WIDTH = DIFF_WIDTH + SB_WIDTH
D_FF = 4 * D_MODEL
EPS = 1e-6
LAYER_IDX = 0
LAM_INIT = 0.8 - 0.6 * math.exp(-0.3 * LAYER_IDX)
ATTN_SCALE = 1.0 / math.sqrt(HEAD_DIM)

LANES = 128
ATTN_TILE = 256
TOKEN_TILE = 512
FF_CHUNK = 1024
MASKED = -1e30
VMEM_LIMIT_BYTES = 56 * 1024 * 1024

_F32 = jnp.float32
_BF16 = jnp.bfloat16


def _rmsnorm(x, gain):
    inv = lax.rsqrt(jnp.mean(x * x, axis=-1, keepdims=True) + EPS)
    return x * inv * gain


def _dot(a, b):
    return jnp.dot(a, b, preferred_element_type=_F32)


def _dot_nt(a, b):
    return lax.dot_general(a, b, (((1,), (1,)), ((), ())), preferred_element_type=_F32)


def _in_proj_kernel(x_ref, g_ref, w_ref, o_ref):
    a = _rmsnorm(x_ref[...], g_ref[...])
    o_ref[...] = _dot(a.astype(_BF16), w_ref[...]).astype(_BF16)


def _in_proj(x2d, gain, w_bf16):
    n_tok = x2d.shape[0]
    n_out = w_bf16.shape[1]
    return pl.pallas_call(
        _in_proj_kernel,
        grid=(n_tok // TOKEN_TILE,),
        in_specs=[
            pl.BlockSpec((TOKEN_TILE, D_MODEL), lambda i: (i, 0)),
            pl.BlockSpec((1, D_MODEL), lambda i: (0, 0)),
            pl.BlockSpec((D_MODEL, n_out), lambda i: (0, 0)),
        ],
        out_specs=pl.BlockSpec((TOKEN_TILE, n_out), lambda i: (i, 0)),
        out_shape=jax.ShapeDtypeStruct((n_tok, n_out), _BF16),
        compiler_params=pltpu.CompilerParams(vmem_limit_bytes=VMEM_LIMIT_BYTES),
        name="in_proj",
    )(x2d, gain, w_bf16)


def _half_masks():
    lane = lax.broadcasted_iota(jnp.int32, (1, LANES), 1)
    return lane < HEAD_DIM, lane >= HEAD_DIM


def _split_halves(x_bf16, scale):
    lo, hi = _half_masks()
    xf = x_bf16.astype(_F32) * scale
    return (jnp.where(lo, xf, 0.0).astype(_BF16), jnp.where(hi, xf, 0.0).astype(_BF16))


def _tile(ref, blk, group):
    rows = pl.ds(pl.multiple_of(blk * ATTN_TILE, ATTN_TILE), ATTN_TILE)
    return ref[rows, group * LANES:(group + 1) * LANES]


def _diff_attn_kernel(slopes_ref, lq1_ref, lk1_ref, lq2_ref, lk2_ref, g_ref,
                      q_ref, k_ref, v_ref, o_ref, m_ref, l_ref, a_ref, dbias_ref):
    t = ATTN_TILE
    reps = t // LANES
    n_blk = q_ref.shape[0] // t
    n_h = q_ref.shape[1] // LANES

    lam = (jnp.exp(jnp.sum(lq1_ref[...] * lk1_ref[...], axis=-1, keepdims=True))
           - jnp.exp(jnp.sum(lq2_ref[...] * lk2_ref[...], axis=-1, keepdims=True))
           + LAM_INIT)

    row = lax.broadcasted_iota(jnp.int32, (t, t), 0)
    col = lax.broadcasted_iota(jnp.int32, (t, t), 1)
    colf = lax.broadcasted_iota(jnp.int32, (1, t), 1).astype(_F32)
    slopes = [slopes_ref[h] for h in range(n_h)]
    for h in range(n_h):
        dbias_ref[h] = jnp.where(row >= col, slopes[h] * col.astype(_F32), MASKED)

    def tile_step(q_halves, kj, first):
        for h in range(n_h):
            k = _tile(k_ref, kj, h)
            v = _tile(v_ref, kj, h)
            bias = dbias_ref[h] if first else slopes[h] * colf
            off = (slopes[h] * t) * kj.astype(_F32)
            for half in range(2):
                idx = 2 * h + half
                s = _dot_nt(q_halves[h][half], k) + bias
                m_curr = jnp.max(s, axis=1, keepdims=True) + off
                if first:
                    m_new = jnp.broadcast_to(m_curr, (t, LANES))
                else:
                    m_prev = m_ref[idx]
                    m_new = jnp.maximum(m_prev, m_curr)
                p = jnp.exp(s - pltpu.repeat(m_new - off, reps, 1))
                p_sum = jnp.sum(p, axis=1, keepdims=True)
                pv = _dot(p.astype(_BF16), v)
                if first:
                    l_ref[idx] = jnp.broadcast_to(p_sum, (t, LANES))
                    a_ref[idx] = pv
                else:
                    alpha = jnp.exp(m_prev - m_new)
                    l_ref[idx] = alpha * l_ref[idx] + p_sum
                    a_ref[idx] = alpha * a_ref[idx] + pv
                m_ref[idx] = m_new

    def q_block(qi, carry):
        q_halves = [_split_halves(_tile(q_ref, qi, h), ATTN_SCALE) for h in range(n_h)]
        tile_step(q_halves, qi, True)

        def k_block(kj, c):
            tile_step(q_halves, kj, False)
            return c

        lax.fori_loop(0, qi, k_block, 0)

        rows = pl.ds(pl.multiple_of(qi * t, t), t)
        for h in range(n_h):
            o = a_ref[2 * h] / l_ref[2 * h] - lam * (a_ref[2 * h + 1] / l_ref[2 * h + 1])
            o = _rmsnorm(o, g_ref[...]) * (1.0 - LAM_INIT)
            o_ref[rows, h * LANES:(h + 1) * LANES] = o.astype(o_ref.dtype)
        return carry

    lax.fori_loop(0, n_blk, q_block, 0)


def _diff_attn(qkv, slopes, lq1, lk1, lq2, lk2, gain):
    b, s, _ = qkv.shape
    n_h = N_DIFF_HEADS
    vec = lambda n: pl.BlockSpec((1, n), lambda bi: (0, 0))
    blk = lambda j: pl.BlockSpec((None, s, DIFF_WIDTH), lambda bi: (bi, 0, j))
    stat = pltpu.VMEM((2 * n_h, ATTN_TILE, LANES), _F32)
    return pl.pallas_call(
        _diff_attn_kernel,
        grid=(b,),
        in_specs=[
            pl.BlockSpec(memory_space=pltpu.SMEM),
            vec(HEAD_DIM), vec(HEAD_DIM), vec(HEAD_DIM), vec(HEAD_DIM), vec(LANES),
            blk(0), blk(1), blk(2),
        ],
        out_specs=pl.BlockSpec((None, s, DIFF_WIDTH), lambda bi: (bi, 0, 0)),
        out_shape=jax.ShapeDtypeStruct((b, s, DIFF_WIDTH), _BF16),
        scratch_shapes=[stat, stat, stat,
                        pltpu.VMEM((n_h, ATTN_TILE, ATTN_TILE), _F32)],
        compiler_params=pltpu.CompilerParams(vmem_limit_bytes=VMEM_LIMIT_BYTES),
        name="diff_attn",
    )(slopes, lq1, lk1, lq2, lk2, gain, qkv, qkv, qkv)


def _sb_attn_kernel(g_ref, q_ref, k_ref, v_ref, o_ref,
                    c_ref, acc_ref, tri_ref, dmask_ref):
    t = ATTN_TILE
    reps = t // LANES
    n_blk = q_ref.shape[0] // t
    n_p = q_ref.shape[1] // LANES

    row = lax.broadcasted_iota(jnp.int32, (t, t), 0)
    col = lax.broadcasted_iota(jnp.int32, (t, t), 1)
    tri_ref[...] = jnp.where(row > col, 1.0, 0.0).astype(_BF16)
    dmask_ref[...] = jnp.where(col < row, 0.0, MASKED)
    lo, hi = _half_masks()

    def tile_step(q_halves, kj, first):
        for p in range(n_p):
            k = _tile(k_ref, kj, p)
            v_halves = _split_halves(_tile(v_ref, kj, p), 1.0)
            pv = None
            for half in range(2):
                idx = 2 * p + half
                z = _dot_nt(q_halves[p][half], k)
                if first:
                    z = z + dmask_ref[...]
                sp = jnp.log(1.0 + jnp.exp(-jnp.abs(z)))
                log_beta = jnp.minimum(z, 0.0) - sp
                log_om = log_beta - z
                later = _dot(log_om.astype(_BF16), tri_ref[...])
                om_sum = jnp.sum(log_om, axis=1, keepdims=True)
                if first:
                    w = jnp.exp(log_beta + later)
                    c_ref[idx] = jnp.broadcast_to(om_sum, (t, LANES))
                else:
                    carry = c_ref[idx]
                    w = jnp.exp(log_beta + later + pltpu.repeat(carry, reps, 1))
                    c_ref[idx] = carry + om_sum
                d = _dot(w.astype(_BF16), v_halves[half])
                pv = d if pv is None else pv + d
            if first:
                acc_ref[p] = pv
            else:
                acc_ref[p] += pv

    def q_block(qi, carry):
        q_halves = [_split_halves(_tile(q_ref, qi, p), ATTN_SCALE) for p in range(n_p)]
        tile_step(q_halves, qi, True)

        def k_block(i, c):
            tile_step(q_halves, qi - 1 - i, False)
            return c

        lax.fori_loop(0, qi, k_block, 0)

        rows = pl.ds(pl.multiple_of(qi * t, t), t)
        for p in range(n_p):
            o = acc_ref[p]
            sq = o * o
            ms_lo = jnp.sum(jnp.where(lo, sq, 0.0), axis=-1, keepdims=True) * (1.0 / HEAD_DIM)
            ms_hi = jnp.sum(jnp.where(hi, sq, 0.0), axis=-1, keepdims=True) * (1.0 / HEAD_DIM)
            inv = jnp.where(lo, lax.rsqrt(ms_lo + EPS), lax.rsqrt(ms_hi + EPS))
            o = o * inv * g_ref[...]
            o_ref[rows, p * LANES:(p + 1) * LANES] = o.astype(o_ref.dtype)
        return carry

    lax.fori_loop(0, n_blk, q_block, 0)


def _sb_attn(qkv, gain2):
    b, s, _ = qkv.shape
    first = 3 * DIFF_WIDTH // SB_WIDTH
    blk = lambda j: pl.BlockSpec((None, s, SB_WIDTH), lambda bi: (bi, 0, first + j))
    return pl.pallas_call(
        _sb_attn_kernel,
        grid=(b,),
        in_specs=[
            pl.BlockSpec((1, LANES), lambda bi: (0, 0)),
            blk(0), blk(1), blk(2),
        ],
        out_specs=pl.BlockSpec((None, s, SB_WIDTH), lambda bi: (bi, 0, 0)),
        out_shape=jax.ShapeDtypeStruct((b, s, SB_WIDTH), _BF16),
        scratch_shapes=[pltpu.VMEM((N_SB_HEADS, ATTN_TILE, LANES), _F32),
                        pltpu.VMEM((N_SB_PAIRS, ATTN_TILE, LANES), _F32),
                        pltpu.VMEM((ATTN_TILE, ATTN_TILE), _BF16),
                        pltpu.VMEM((ATTN_TILE, ATTN_TILE), _F32)],
        compiler_params=pltpu.CompilerParams(vmem_limit_bytes=VMEM_LIMIT_BYTES),
        name="sb_attn",
    )(gain2, qkv, qkv, qkv)


def _out_mlp_kernel(x_ref, md_ref, ms_ref, wod_ref, wos_ref, g2_ref, wup_ref, wdn_ref,
                    g3_ref, o_ref, m_ref, acc_ref):
    h = x_ref[...] + _dot(md_ref[...], wod_ref[...]) + _dot(ms_ref[...], wos_ref[...])
    m_ref[...] = _rmsnorm(h, g2_ref[...]).astype(_BF16)
    acc_ref[...] = h

    def ff_chunk(c, carry):
        cols = pl.ds(pl.multiple_of(c * FF_CHUNK, FF_CHUNK), FF_CHUNK)
        u = jnp.square(jnp.maximum(_dot(m_ref[...], wup_ref[:, cols]), 0.0))
        acc_ref[...] += _dot(u.astype(_BF16), wdn_ref[cols, :])
        return carry

    lax.fori_loop(0, D_FF // FF_CHUNK, ff_chunk, 0)
    o_ref[...] = _rmsnorm(acc_ref[...], g3_ref[...])


def _out_mlp(x2d, mix_d, mix_s, wo_d, wo_s, g2, w_up, w_dn, g3):
    n_tok = x2d.shape[0]
    const = lambda shape: pl.BlockSpec(shape, lambda i: (0, 0), pipeline_mode=pl.Buffered(1))
    return pl.pallas_call(
        _out_mlp_kernel,
        grid=(n_tok // TOKEN_TILE,),
        in_specs=[
            pl.BlockSpec((TOKEN_TILE, D_MODEL), lambda i: (i, 0)),
            pl.BlockSpec((TOKEN_TILE, DIFF_WIDTH), lambda i: (i, 0)),
            pl.BlockSpec((TOKEN_TILE, SB_WIDTH), lambda i: (i, 0)),
            const((DIFF_WIDTH, D_MODEL)),
            const((SB_WIDTH, D_MODEL)),
            const((1, D_MODEL)),
            const((D_MODEL, D_FF)),
            const((D_FF, D_MODEL)),
            const((1, D_MODEL)),
        ],
        out_specs=pl.BlockSpec((TOKEN_TILE, D_MODEL), lambda i: (i, 0)),
        out_shape=jax.ShapeDtypeStruct((n_tok, D_MODEL), _F32),
        scratch_shapes=[pltpu.VMEM((TOKEN_TILE, D_MODEL), _BF16),
                        pltpu.VMEM((TOKEN_TILE, D_MODEL), _F32)],
        compiler_params=pltpu.CompilerParams(vmem_limit_bytes=VMEM_LIMIT_BYTES),
        name="out_mlp",
    )(x2d, mix_d, mix_s, wo_d, wo_s, g2, w_up, w_dn, g3)


def kernel(x, attn_norm, w_in, lambda_q1, lambda_k1, lambda_q2, lambda_k2, diff_subln, sb_subln,
           w_out, mlp_norm, w_up, w_down, final_norm):
    assert attn_norm.shape[0] == 1, "single-layer block"
    b, s, d = x.shape
    x2d = x.reshape(b * s, d)

    qkv = _in_proj(x2d, attn_norm, w_in[0].astype(_BF16)).reshape(b, s, 3 * MIX_WIDTH)

    slopes = jnp.asarray([2.0 ** (-8.0 * (h + 1) / N_DIFF_HEADS) for h in range(N_DIFF_HEADS)],
                         dtype=_F32)
    mix_d = _diff_attn(qkv, slopes, lambda_q1, lambda_k1, lambda_q2, lambda_k2, diff_subln)
    mix_s = _sb_attn(qkv, jnp.tile(sb_subln, (1, 2)))

    w_out_bf16 = w_out[0].astype(_BF16)
    out = _out_mlp(x2d, mix_d.reshape(b * s, DIFF_WIDTH), mix_s.reshape(b * s, SB_WIDTH),
                   w_out_bf16[:DIFF_WIDTH], w_out_bf16[DIFF_WIDTH:], mlp_norm,
                   w_up[0].astype(_BF16), w_down[0].astype(_BF16), final_norm.reshape(1, d))
    return out.reshape(b, s, d)
```

```python
import math

import jax
import jax.numpy as jnp
from jax import lax
from jax.experimental import pallas as pl
from jax.experimental.pallas import tpu as pltpu

D_MODEL = 1024
HEAD_DIM = 64
DIFF_WIDTH = D_MODEL // 2
N_DIFF_HEADS = DIFF_WIDTH // (2 * HEAD_DIM)
SB_WIDTH = D_MODEL - DIFF_WIDTH
N_SB_HEADS = SB_WIDTH // HEAD_DIM
N_SB_PAIRS = N_SB_HEADS // 2
MIX_WIDTH = DIFF_WIDTH + SB_WIDTH
D_FF = 4 * D_MODEL
EPS = 1e-6
LAYER_IDX = 0
LAM_INIT = 0.8 - 0.6 * math.exp(-0.3 * LAYER_IDX)
ATTN_SCALE = 1.0 / math.sqrt(HEAD_DIM)
LOG2E = math.log2(math.e)

LANES = 128
ATTN_TILE = 256
TOKEN_TILE = 512
FF_CHUNK = 1024
MASKED = -1e30
VMEM_LIMIT_BYTES = 56 * 1024 * 1024

_F32 = jnp.float32
_BF16 = jnp.bfloat16


def _rmsnorm(x, gain):
    inv = lax.rsqrt(jnp.mean(x * x, axis=-1, keepdims=True) + EPS)
    return x * inv * gain


def _dot(a, b):
    return jnp.dot(a, b, preferred_element_type=_F32)


def _dot_nt(a, b):
    return lax.dot_general(a, b, (((1,), (1,)), ((), ())), preferred_element_type=_F32)


def _in_proj_kernel(x_ref, g_ref, w_ref, cs_ref, o_ref):
    a = _rmsnorm(x_ref[...], g_ref[...])
    o_ref[...] = (_dot(a.astype(_BF16), w_ref[...]) * cs_ref[...]).astype(_BF16)


def _in_proj(x2d, gain, w_bf16, col_scale):
    n_tok = x2d.shape[0]
    n_out = w_bf16.shape[1]
    return pl.pallas_call(
        _in_proj_kernel,
        grid=(n_tok // TOKEN_TILE,),
        in_specs=[
            pl.BlockSpec((TOKEN_TILE, D_MODEL), lambda i: (i, 0)),
            pl.BlockSpec((1, D_MODEL), lambda i: (0, 0)),
            pl.BlockSpec((D_MODEL, n_out), lambda i: (0, 0)),
            pl.BlockSpec((1, n_out), lambda i: (0, 0)),
        ],
        out_specs=pl.BlockSpec((TOKEN_TILE, n_out), lambda i: (i, 0)),
        out_shape=jax.ShapeDtypeStruct((n_tok, n_out), _BF16),
        compiler_params=pltpu.CompilerParams(vmem_limit_bytes=VMEM_LIMIT_BYTES),
        name="in_proj",
    )(x2d, gain, w_bf16, col_scale)


def _split_halves(x_bf16):
    lane = lax.broadcasted_iota(jnp.int32, (1, LANES), 1)
    zero = jnp.zeros_like(x_bf16)
    return (jnp.where(lane < HEAD_DIM, x_bf16, zero), jnp.where(lane >= HEAD_DIM, x_bf16, zero))


def _tile(ref, blk, group):
    rows = pl.ds(pl.multiple_of(blk * ATTN_TILE, ATTN_TILE), ATTN_TILE)
    return ref[rows, group * LANES:(group + 1) * LANES]


def _as_f32(i):
    return jnp.asarray(i, jnp.int32).astype(_F32)


def _diff_attn_kernel(slopes_ref, lq1_ref, lk1_ref, lq2_ref, lk2_ref, g_ref,
                      q_ref, k_ref, v_ref, o_ref, m_ref, a_ref, dbias_ref):
    t = ATTN_TILE
    reps = t // LANES
    n_blk = q_ref.shape[0] // t
    n_h = q_ref.shape[1] // LANES

    lam = (jnp.exp(jnp.sum(lq1_ref[...] * lk1_ref[...], axis=-1, keepdims=True))
           - jnp.exp(jnp.sum(lq2_ref[...] * lk2_ref[...], axis=-1, keepdims=True))
           + LAM_INIT)

    row = lax.broadcasted_iota(jnp.int32, (t, t), 0)
    col = lax.broadcasted_iota(jnp.int32, (t, t), 1)
    colf = lax.broadcasted_iota(jnp.int32, (1, t), 1).astype(_F32)
    slopes = [slopes_ref[h] for h in range(n_h)]
    for h in range(n_h):
        dbias_ref[h] = jnp.where(row >= col, slopes[h] * col.astype(_F32), MASKED)
    ones = jnp.ones((t, LANES), _BF16)

    def tile_step(q_halves, kj, tiles_back, first):
        for h in range(n_h):
            k = _tile(k_ref, kj, h)
            v_aug = jnp.concatenate([_tile(v_ref, kj, h), ones], axis=1)
            if first:
                bias = dbias_ref[h]
            else:
                bias = slopes[h] * (colf - t * _as_f32(tiles_back))
            for half in range(2):
                idx = 2 * h + half
                s = _dot_nt(q_halves[h][half], k) + bias
                m_curr = jnp.max(s, axis=1, keepdims=True)
                if first:
                    m_new = jnp.broadcast_to(m_curr, (t, LANES))
                else:
                    m_prev = m_ref[idx]
                    m_new = jnp.maximum(m_prev, m_curr)
                p = jnp.exp2(s - pltpu.repeat(m_new, reps, 1))
                pv = _dot(p.astype(_BF16), v_aug)
                if first:
                    a_ref[idx] = pv
                else:
                    alpha = jnp.exp2(m_prev - m_new)
                    a_ref[idx] = pltpu.repeat(alpha, 2, 1) * a_ref[idx] + pv
                m_ref[idx] = m_new

    def q_block(qi, carry):
        q_halves = [_split_halves(_tile(q_ref, qi, h)) for h in range(n_h)]
        tile_step(q_halves, qi, 0, True)

        def k_block(kj, c):
            tile_step(q_halves, kj, qi - kj, False)
            return c

        lax.fori_loop(0, qi, k_block, 0)

        rows = pl.ds(pl.multiple_of(qi * t, t), t)
        for h in range(n_h):
            a1 = a_ref[2 * h]
            a2 = a_ref[2 * h + 1]
            o = a1[:, :LANES] / a1[:, LANES:] - lam * (a2[:, :LANES] / a2[:, LANES:])
            o = _rmsnorm(o, g_ref[...]) * (1.0 - LAM_INIT)
            o_ref[rows, h * LANES:(h + 1) * LANES] = o.astype(o_ref.dtype)
        return carry

    lax.fori_loop(0, n_blk, q_block, 0)


def _diff_attn(qkv, slopes2, lq1, lk1, lq2, lk2, gain):
    b, s, _ = qkv.shape
    n_h = N_DIFF_HEADS
    vec = lambda n: pl.BlockSpec((1, n), lambda bi: (0, 0))
    blk = lambda j: pl.BlockSpec((None, s, DIFF_WIDTH), lambda bi: (bi, 0, j))
    return pl.pallas_call(
        _diff_attn_kernel,
        grid=(b,),
        in_specs=[
            pl.BlockSpec(memory_space=pltpu.SMEM),
            vec(HEAD_DIM), vec(HEAD_DIM), vec(HEAD_DIM), vec(HEAD_DIM), vec(LANES),
            blk(0), blk(1), blk(2),
        ],
        out_specs=pl.BlockSpec((None, s, DIFF_WIDTH), lambda bi: (bi, 0, 0)),
        out_shape=jax.ShapeDtypeStruct((b, s, DIFF_WIDTH), _BF16),
        scratch_shapes=[pltpu.VMEM((2 * n_h, ATTN_TILE, LANES), _F32),
                        pltpu.VMEM((2 * n_h, ATTN_TILE, 2 * LANES), _F32),
                        pltpu.VMEM((n_h, ATTN_TILE, ATTN_TILE), _F32)],
        compiler_params=pltpu.CompilerParams(vmem_limit_bytes=VMEM_LIMIT_BYTES),
        name="diff_attn",
    )(slopes2, lq1, lk1, lq2, lk2, gain, qkv, qkv, qkv)


def _sb_attn_kernel(g_ref, q_ref, k_ref, v_ref, o_ref,
                    c_ref, acc_ref, z_ref, tri_ref, dmask_ref):
    t = ATTN_TILE
    reps = t // LANES
    n_blk = q_ref.shape[0] // t
    n_p = q_ref.shape[1] // LANES

    row = lax.broadcasted_iota(jnp.int32, (t, t), 0)
    col = lax.broadcasted_iota(jnp.int32, (t, t), 1)
    tri_ref[...] = jnp.where(row > col, -1.0, 0.0).astype(_BF16)
    dmask_ref[...] = jnp.where(col < row, 0.0, MASKED)
    lane = lax.broadcasted_iota(jnp.int32, (1, LANES), 1)
    lo, hi = lane < HEAD_DIM, lane >= HEAD_DIM

    def tile_step(q_halves, kj, first):
        for p in range(n_p):
            k = _tile(k_ref, kj, p)
            for half in range(2):
                z_ref[2 * p + half] = _dot_nt(q_halves[p][half], k)
        for p in range(n_p):
            v_halves = _split_halves(_tile(v_ref, kj, p))
            pv = None
            for half in range(2):
                idx = 2 * p + half
                z = z_ref[idx]
                if first:
                    z = z + dmask_ref[...]
                sp = jnp.log(1.0 + jnp.exp2(-jnp.abs(z))) * LOG2E
                neg_log_om = jnp.maximum(z, 0.0) + sp
                log_beta = z - neg_log_om
                later = _dot(neg_log_om.astype(_BF16), tri_ref[...])
                om_sum = jnp.sum(neg_log_om, axis=1, keepdims=True)
                if first:
                    w = jnp.exp2(later + log_beta)
                    c_ref[idx] = jnp.broadcast_to(om_sum, (t, LANES))
                else:
                    carry = c_ref[idx]
                    w = jnp.exp2(later + log_beta - pltpu.repeat(carry, reps, 1))
                    c_ref[idx] = carry + om_sum
                d = _dot(w.astype(_BF16), v_halves[half])
                pv = d if pv is None else pv + d
            if first:
                acc_ref[p] = pv
            else:
                acc_ref[p] += pv

    def q_block(qi, carry):
        q_halves = [_split_halves(_tile(q_ref, qi, p)) for p in range(n_p)]
        tile_step(q_halves, qi, True)

        def k_block(i, c):
            tile_step(q_halves, qi - 1 - i, False)
            return c

        lax.fori_loop(0, qi, k_block, 0)

        rows = pl.ds(pl.multiple_of(qi * t, t), t)
        for p in range(n_p):
            o = acc_ref[p]
            sq = o * o
            ms_lo = jnp.sum(jnp.where(lo, sq, 0.0), axis=-1, keepdims=True) * (1.0 / HEAD_DIM)
            ms_hi = jnp.sum(jnp.where(hi, sq, 0.0), axis=-1, keepdims=True) * (1.0 / HEAD_DIM)
            inv = jnp.where(lo, lax.rsqrt(ms_lo + EPS), lax.rsqrt(ms_hi + EPS))
            o = o * inv * g_ref[...]
            o_ref[rows, p * LANES:(p + 1) * LANES] = o.astype(o_ref.dtype)
        return carry

    lax.fori_loop(0, n_blk, q_block, 0)


def _sb_attn(qkv, gain2):
    b, s, _ = qkv.shape
    first = 3 * DIFF_WIDTH // SB_WIDTH
    blk = lambda j: pl.BlockSpec((None, s, SB_WIDTH), lambda bi: (bi, 0, first + j))
    return pl.pallas_call(
        _sb_attn_kernel,
        grid=(b,),
        in_specs=[
            pl.BlockSpec((1, LANES), lambda bi: (0, 0)),
            blk(0), blk(1), blk(2),
        ],
        out_specs=pl.BlockSpec((None, s, SB_WIDTH), lambda bi: (bi, 0, 0)),
        out_shape=jax.ShapeDtypeStruct((b, s, SB_WIDTH), _BF16),
        scratch_shapes=[pltpu.VMEM((N_SB_HEADS, ATTN_TILE, LANES), _F32),
                        pltpu.VMEM((N_SB_PAIRS, ATTN_TILE, LANES), _F32),
                        pltpu.VMEM((N_SB_HEADS, ATTN_TILE, ATTN_TILE), _F32),
                        pltpu.VMEM((ATTN_TILE, ATTN_TILE), _BF16),
                        pltpu.VMEM((ATTN_TILE, ATTN_TILE), _F32)],
        compiler_params=pltpu.CompilerParams(vmem_limit_bytes=VMEM_LIMIT_BYTES),
        name="sb_attn",
    )(gain2, qkv, qkv, qkv)


def _out_mlp_kernel(x_ref, md_ref, ms_ref, wod_ref, wos_ref, g2_ref, wup_ref, wdn_ref,
                    g3_ref, o_ref, m_ref, acc_ref):
    h = x_ref[...] + _dot(md_ref[...], wod_ref[...]) + _dot(ms_ref[...], wos_ref[...])
    m_ref[...] = _rmsnorm(h, g2_ref[...]).astype(_BF16)
    acc_ref[...] = h

    def ff_chunk(c, carry):
        cols = pl.ds(pl.multiple_of(c * FF_CHUNK, FF_CHUNK), FF_CHUNK)
        u = jnp.square(jnp.maximum(_dot(m_ref[...], wup_ref[:, cols]), 0.0))
        acc_ref[...] += _dot(u.astype(_BF16), wdn_ref[cols, :])
        return carry

    lax.fori_loop(0, D_FF // FF_CHUNK, ff_chunk, 0)
    o_ref[...] = _rmsnorm(acc_ref[...], g3_ref[...])


def _out_mlp(x2d, mix_d, mix_s, wo_d, wo_s, g2, w_up, w_dn, g3):
    n_tok = x2d.shape[0]
    const = lambda shape: pl.BlockSpec(shape, lambda i: (0, 0), pipeline_mode=pl.Buffered(1))
    return pl.pallas_call(
        _out_mlp_kernel,
        grid=(n_tok // TOKEN_TILE,),
        in_specs=[
            pl.BlockSpec((TOKEN_TILE, D_MODEL), lambda i: (i, 0)),
            pl.BlockSpec((TOKEN_TILE, DIFF_WIDTH), lambda i: (i, 0)),
            pl.BlockSpec((TOKEN_TILE, SB_WIDTH), lambda i: (i, 0)),
            const((DIFF_WIDTH, D_MODEL)),
            const((SB_WIDTH, D_MODEL)),
            const((1, D_MODEL)),
            const((D_MODEL, D_FF)),
            const((D_FF, D_MODEL)),
            const((1, D_MODEL)),
        ],
        out_specs=pl.BlockSpec((TOKEN_TILE, D_MODEL), lambda i: (i, 0)),
        out_shape=jax.ShapeDtypeStruct((n_tok, D_MODEL), _F32),
        scratch_shapes=[pltpu.VMEM((TOKEN_TILE, D_MODEL), _BF16),
                        pltpu.VMEM((TOKEN_TILE, D_MODEL), _F32)],
        compiler_params=pltpu.CompilerParams(vmem_limit_bytes=VMEM_LIMIT_BYTES),
        name="out_mlp",
    )(x2d, mix_d, mix_s, wo_d, wo_s, g2, w_up, w_dn, g3)


def _query_col_scale():
    qs = ATTN_SCALE * LOG2E
    one = jnp.ones((DIFF_WIDTH,), _F32)
    return jnp.concatenate([qs * one, one, one, qs * one, one, one]).reshape(1, 3 * MIX_WIDTH)


def kernel(x, attn_norm, w_in, lambda_q1, lambda_k1, lambda_q2, lambda_k2, diff_subln, sb_subln,
           w_out, mlp_norm, w_up, w_down, final_norm):
    assert attn_norm.shape[0] == 1, "single-layer block"
    b, s, d = x.shape
    x2d = x.reshape(b * s, d)

    qkv = _in_proj(x2d, attn_norm, w_in[0].astype(_BF16), _query_col_scale())
    qkv = qkv.reshape(b, s, 3 * MIX_WIDTH)

    slopes2 = jnp.asarray([LOG2E * 2.0 ** (-8.0 * (h + 1) / N_DIFF_HEADS)
                           for h in range(N_DIFF_HEADS)], dtype=_F32)
    mix_d = _diff_attn(qkv, slopes2, lambda_q1, lambda_k1, lambda_q2, lambda_k2, diff_subln)
    mix_s = _sb_attn(qkv, jnp.tile(sb_subln, (1, 2)))

    w_out_bf16 = w_out[0].astype(_BF16)
    out = _out_mlp(x2d, mix_d.reshape(b * s, DIFF_WIDTH), mix_s.reshape(b * s, SB_WIDTH),
                   w_out_bf16[:DIFF_WIDTH], w_out_bf16[DIFF_WIDTH:], mlp_norm,
                   w_up[0].astype(_BF16), w_down[0].astype(_BF16), final_norm.reshape(1, d))
    return out.reshape(b, s, d)
```

```python
import math

import jax
import jax.numpy as jnp
import numpy as np
from jax import lax
from jax.experimental import pallas as pl
from jax.experimental.pallas import tpu as pltpu

D_MODEL = 1024
HEAD_DIM = 64
DIFF_WIDTH = D_MODEL // 2
N_DIFF_HEADS = DIFF_WIDTH // (2 * HEAD_DIM)
SB_WIDTH = D_MODEL - DIFF_WIDTH
N_SB_HEADS = SB_WIDTH // HEAD_DIM
N_SB_PAIRS = N_SB_HEADS // 2
MIX_WIDTH = DIFF_WIDTH + SB_WIDTH
D_FF = 4 * D_MODEL
EPS = 1e-6
LAYER_IDX = 0
LAM_INIT = 0.8 - 0.6 * math.exp(-0.3 * LAYER_IDX)
ATTN_SCALE = 1.0 / math.sqrt(HEAD_DIM)
LOG2E = math.log2(math.e)

LANES = 128
GROUPS = 4
ATTN_TILE = 256
TOKEN_TILE = 512
FF_CHUNK = 1024
MASKED = -1e30
VMEM_LIMIT_BYTES = 56 * 1024 * 1024

_F32 = jnp.float32
_BF16 = jnp.bfloat16


def _rmsnorm(x, gain):
    inv = lax.rsqrt(jnp.mean(x * x, axis=-1, keepdims=True) + EPS)
    return x * inv * gain


def _dot(a, b):
    return jnp.dot(a, b, preferred_element_type=_F32)


def _dot_nt(a, b):
    return lax.dot_general(a, b, (((1,), (1,)), ((), ())), preferred_element_type=_F32)


def _lane_tile(x, reps):
    return jnp.concatenate([x] * reps, axis=1)


def _in_proj_kernel(x_ref, g_ref, w_ref, cs_ref, o_ref):
    a = _rmsnorm(x_ref[...], g_ref[...])
    res = (_dot(a.astype(_BF16), w_ref[...]) * cs_ref[...]).astype(_BF16)
    for g in range(o_ref.shape[0]):
        o_ref[g] = res[:, g * LANES:(g + 1) * LANES]


def _in_proj(x2d, gain, w_bf16, col_scale):
    n_tok = x2d.shape[0]
    n_out = w_bf16.shape[1]
    n_groups = n_out // LANES
    return pl.pallas_call(
        _in_proj_kernel,
        grid=(n_tok // TOKEN_TILE,),
        in_specs=[
            pl.BlockSpec((TOKEN_TILE, D_MODEL), lambda i: (i, 0)),
            pl.BlockSpec((1, D_MODEL), lambda i: (0, 0)),
            pl.BlockSpec((D_MODEL, n_out), lambda i: (0, 0)),
            pl.BlockSpec((1, n_out), lambda i: (0, 0)),
        ],
        out_specs=pl.BlockSpec((n_groups, TOKEN_TILE, LANES), lambda i: (0, i, 0)),
        out_shape=jax.ShapeDtypeStruct((n_groups, n_tok, LANES), _BF16),
        compiler_params=pltpu.CompilerParams(vmem_limit_bytes=VMEM_LIMIT_BYTES),
        name="in_proj",
    )(x2d, gain, w_bf16, col_scale)


def _split_halves(x_bf16):
    lane = lax.broadcasted_iota(jnp.int32, (1, LANES), 1)
    zero = jnp.zeros_like(x_bf16)
    return (jnp.where(lane < HEAD_DIM, x_bf16, zero), jnp.where(lane >= HEAD_DIM, x_bf16, zero))


def _rows(blk):
    start = blk * ATTN_TILE
    if not isinstance(start, int):
        start = pl.multiple_of(start, ATTN_TILE)
    return pl.ds(start, ATTN_TILE)


def _as_f32(i):
    return jnp.asarray(i, jnp.int32).astype(_F32)


def _group_spec(seq, section):
    return pl.BlockSpec((GROUPS, seq, LANES), lambda bi: (section, bi, 0))


def _diff_attn_kernel(slopes_ref, lq1_ref, lk1_ref, lq2_ref, lk2_ref, g_ref,
                      q_ref, k_ref, v_ref, o_ref, m_ref, a_ref, dbias_ref):
    t = ATTN_TILE
    reps = t // LANES
    n_blk = q_ref.shape[1] // t
    n_h = q_ref.shape[0]

    lam = (jnp.exp(jnp.sum(lq1_ref[...] * lk1_ref[...], axis=-1, keepdims=True))
           - jnp.exp(jnp.sum(lq2_ref[...] * lk2_ref[...], axis=-1, keepdims=True))
           + LAM_INIT)

    row = lax.broadcasted_iota(jnp.int32, (t, t), 0)
    col = lax.broadcasted_iota(jnp.int32, (t, t), 1)
    colf = lax.broadcasted_iota(jnp.int32, (1, t), 1).astype(_F32)
    slopes = [slopes_ref[h] for h in range(n_h)]
    for h in range(n_h):
        dbias_ref[h] = jnp.where(row >= col, slopes[h] * col.astype(_F32), MASKED)
    ones = jnp.ones((t, LANES), _BF16)

    def tile_step(q_halves, kj, tiles_back, first):
        for h in range(n_h):
            k = k_ref[h, _rows(kj), :]
            v_aug = jnp.concatenate([v_ref[h, _rows(kj), :], ones], axis=1)
            if first:
                bias = dbias_ref[h]
            else:
                bias = slopes[h] * (colf - t * _as_f32(tiles_back))
            for half in range(2):
                idx = 2 * h + half
                s = _dot_nt(q_halves[h][half], k) + bias
                m_curr = jnp.max(s, axis=1, keepdims=True)
                if first:
                    m_new = jnp.broadcast_to(m_curr, (t, LANES))
                else:
                    m_prev = m_ref[idx]
                    m_new = jnp.maximum(m_prev, m_curr)
                p = jnp.exp2(s - _lane_tile(m_new, reps))
                pv = _dot(p.astype(_BF16), v_aug)
                if first:
                    a_ref[idx] = pv
                else:
                    alpha = jnp.exp2(m_prev - m_new)
                    a_ref[idx] = _lane_tile(alpha, 2) * a_ref[idx] + pv
                m_ref[idx] = m_new

    def q_block(qi, carry):
        q_halves = [_split_halves(q_ref[h, _rows(qi), :]) for h in range(n_h)]
        tile_step(q_halves, qi, 0, True)

        def k_block(kj, c):
            tile_step(q_halves, kj, qi - kj, False)
            return c

        lax.fori_loop(0, qi, k_block, 0)

        for h in range(n_h):
            a1 = a_ref[2 * h]
            a2 = a_ref[2 * h + 1]
            o = a1[:, :LANES] / a1[:, LANES:] - lam * (a2[:, :LANES] / a2[:, LANES:])
            o = _rmsnorm(o, g_ref[...]) * (1.0 - LAM_INIT)
            o_ref[h, _rows(qi), :] = o.astype(o_ref.dtype)
        return carry

    lax.fori_loop(0, n_blk, q_block, 0)


def _diff_attn(qkv_g, batch, slopes2, lq1, lk1, lq2, lk2, gain):
    n_tok = qkv_g.shape[1]
    s = n_tok // batch
    n_h = N_DIFF_HEADS
    vec = lambda n: pl.BlockSpec((1, n), lambda bi: (0, 0))
    return pl.pallas_call(
        _diff_attn_kernel,
        grid=(batch,),
        in_specs=[
            pl.BlockSpec(memory_space=pltpu.SMEM),
            vec(HEAD_DIM), vec(HEAD_DIM), vec(HEAD_DIM), vec(HEAD_DIM), vec(LANES),
            _group_spec(s, 0), _group_spec(s, 1), _group_spec(s, 2),
        ],
        out_specs=pl.BlockSpec((GROUPS, s, LANES), lambda bi: (0, bi, 0)),
        out_shape=jax.ShapeDtypeStruct((GROUPS, n_tok, LANES), _BF16),
        scratch_shapes=[pltpu.VMEM((2 * n_h, ATTN_TILE, LANES), _F32),
                        pltpu.VMEM((2 * n_h, ATTN_TILE, 2 * LANES), _F32),
                        pltpu.VMEM((n_h, ATTN_TILE, ATTN_TILE), _F32)],
        compiler_params=pltpu.CompilerParams(vmem_limit_bytes=VMEM_LIMIT_BYTES),
        name="diff_attn",
    )(slopes2, lq1, lk1, lq2, lk2, gain, qkv_g, qkv_g, qkv_g)


def _skewed_pipeline(n_units, stages):
    depth = len(stages)
    assert n_units >= depth

    def iteration(j, parity, lo, hi):
        for k in range(depth - 1, -1, -1):
            if lo <= k < hi:
                stages[k](j - k, (parity - k) % 2)

    for j in range(depth - 1):
        iteration(j, j % 2, 0, j + 1)

    start = depth - 1
    n_pairs, odd = divmod(n_units - start, 2)

    def body(i, c):
        j = start + 2 * i
        iteration(j, start % 2, 0, depth)
        iteration(j + 1, (start + 1) % 2, 0, depth)
        return c

    lax.fori_loop(0, n_pairs, body, 0)
    if odd:
        iteration(n_units - 1, (n_units - 1) % 2, 0, depth)
    for j in range(n_units, n_units + depth - 1):
        iteration(j, j % 2, j - n_units + 1, depth)


def _sb_attn_kernel(steps_q_ref, steps_k_ref, g_ref, q_ref, k_ref, v_ref, o_ref,
                    c_ref, acc_ref, z_ref, x_ref, lb_ref, w_ref, tri_ref, dmask_ref):
    t = ATTN_TILE
    reps = t // LANES
    n_blk = q_ref.shape[1] // t
    n_g = q_ref.shape[0]
    g_shift = n_g.bit_length() - 1
    assert n_g == 1 << g_shift

    row = lax.broadcasted_iota(jnp.int32, (t, t), 0)
    col = lax.broadcasted_iota(jnp.int32, (t, t), 1)
    tri_ref[...] = jnp.where(row > col, -1.0, 0.0).astype(_BF16)
    dmask_ref[...] = jnp.where(col < row, 0.0, MASKED)
    lane = lax.broadcasted_iota(jnp.int32, (1, LANES), 1)
    lo, hi = lane < HEAD_DIM, lane >= HEAD_DIM

    def make_stages(diagonal):
        def decode(u):
            step = u >> g_shift
            grp = u & (n_g - 1)
            if diagonal:
                return step, step, grp
            return steps_q_ref[step], steps_k_ref[step], grp

        def logits(u, slot):
            qi, kj, grp = decode(u)
            q_halves = _split_halves(q_ref[grp, _rows(qi), :])
            k = k_ref[grp, _rows(kj), :]
            for half in range(2):
                z = _dot_nt(q_halves[half], k)
                if diagonal:
                    z = z + dmask_ref[...]
                z_ref[slot, half] = z

        def log_terms(u, slot):
            qi, kj, grp = decode(u)
            for half in range(2):
                head = 2 * grp + half
                z = z_ref[slot, half]
                sp = jnp.log(1.0 + jnp.exp2(-jnp.abs(z))) * LOG2E
                neg_log_om = jnp.maximum(z, 0.0) + sp
                log_beta = z - neg_log_om
                om_sum = jnp.sum(neg_log_om, axis=1, keepdims=True)
                x_ref[slot, half] = neg_log_om.astype(_BF16)
                if diagonal:
                    lb_ref[slot, half] = log_beta
                    c_ref[qi, head] = jnp.broadcast_to(om_sum, (t, LANES))
                else:
                    carry = c_ref[qi, head]
                    lb_ref[slot, half] = log_beta - _lane_tile(carry, reps)
                    c_ref[qi, head] = carry + om_sum

        def weights(u, slot):
            for half in range(2):
                later = _dot(x_ref[slot, half], tri_ref[...])
                w_ref[slot, half] = jnp.exp2(later + lb_ref[slot, half]).astype(_BF16)

        def values(u, slot):
            qi, kj, grp = decode(u)
            v_halves = _split_halves(v_ref[grp, _rows(kj), :])
            pv = (_dot(w_ref[slot, 0], v_halves[0])
                  + _dot(w_ref[slot, 1], v_halves[1]))
            if diagonal:
                acc_ref[qi, grp] = pv
            else:
                acc_ref[qi, grp] += pv

        return logits, log_terms, weights, values

    _skewed_pipeline(n_blk * n_g, make_stages(True))
    _skewed_pipeline(steps_q_ref.shape[0] * n_g, make_stages(False))

    def finish(qi, carry):
        for grp in range(n_g):
            o = acc_ref[qi, grp]
            sq = o * o
            ms_lo = jnp.sum(jnp.where(lo, sq, 0.0), axis=-1, keepdims=True) * (1.0 / HEAD_DIM)
            ms_hi = jnp.sum(jnp.where(hi, sq, 0.0), axis=-1, keepdims=True) * (1.0 / HEAD_DIM)
            inv = jnp.where(lo, lax.rsqrt(ms_lo + EPS), lax.rsqrt(ms_hi + EPS))
            o_ref[grp, _rows(qi), :] = (o * inv * g_ref[...]).astype(o_ref.dtype)
        return carry

    lax.fori_loop(0, n_blk, finish, 0)


def _sb_attn(qkv_g, batch, gain2):
    n_tok = qkv_g.shape[1]
    s = n_tok // batch
    n_blk = s // ATTN_TILE
    steps = [(qi, kj) for qi in range(1, n_blk) for kj in range(qi - 1, -1, -1)]
    steps_q = jnp.asarray(np.array([qk[0] for qk in steps], np.int32))
    steps_k = jnp.asarray(np.array([qk[1] for qk in steps], np.int32))
    smem = pl.BlockSpec(memory_space=pltpu.SMEM)
    return pl.pallas_call(
        _sb_attn_kernel,
        grid=(batch,),
        in_specs=[
            smem, smem,
            pl.BlockSpec((1, LANES), lambda bi: (0, 0)),
            _group_spec(s, 3), _group_spec(s, 4), _group_spec(s, 5),
        ],
        out_specs=pl.BlockSpec((GROUPS, s, LANES), lambda bi: (0, bi, 0)),
        out_shape=jax.ShapeDtypeStruct((GROUPS, n_tok, LANES), _BF16),
        scratch_shapes=[pltpu.VMEM((n_blk, N_SB_HEADS, ATTN_TILE, LANES), _F32),
                        pltpu.VMEM((n_blk, GROUPS, ATTN_TILE, LANES), _F32),
                        pltpu.VMEM((2, 2, ATTN_TILE, ATTN_TILE), _F32),
                        pltpu.VMEM((2, 2, ATTN_TILE, ATTN_TILE), _BF16),
                        pltpu.VMEM((2, 2, ATTN_TILE, ATTN_TILE), _F32),
                        pltpu.VMEM((2, 2, ATTN_TILE, ATTN_TILE), _BF16),
                        pltpu.VMEM((ATTN_TILE, ATTN_TILE), _BF16),
                        pltpu.VMEM((ATTN_TILE, ATTN_TILE), _F32)],
        compiler_params=pltpu.CompilerParams(vmem_limit_bytes=VMEM_LIMIT_BYTES),
        name="sb_attn",
    )(steps_q, steps_k, gain2, qkv_g, qkv_g, qkv_g)


def _out_mlp_kernel(x_ref, md_ref, ms_ref, wo_ref, g2_ref, wup_ref, wdn_ref,
                    g3_ref, o_ref, m_ref, acc_ref):
    mixed = jnp.concatenate([md_ref[g] for g in range(md_ref.shape[0])]
                            + [ms_ref[g] for g in range(ms_ref.shape[0])], axis=1)
    h = x_ref[...] + _dot(mixed, wo_ref[...])
    m_ref[...] = _rmsnorm(h, g2_ref[...]).astype(_BF16)
    acc_ref[...] = h

    def ff_chunk(c, carry):
        cols = pl.ds(pl.multiple_of(c * FF_CHUNK, FF_CHUNK), FF_CHUNK)
        u = jnp.square(jnp.maximum(_dot(m_ref[...], wup_ref[:, cols]), 0.0))
        acc_ref[...] += _dot(u.astype(_BF16), wdn_ref[cols, :])
        return carry

    lax.fori_loop(0, D_FF // FF_CHUNK, ff_chunk, 0)
    o_ref[...] = _rmsnorm(acc_ref[...], g3_ref[...])


def _out_mlp(x2d, mix_d, mix_s, w_out, g2, w_up, w_dn, g3):
    n_tok = x2d.shape[0]
    const = lambda shape: pl.BlockSpec(shape, lambda i: (0, 0), pipeline_mode=pl.Buffered(1))
    mix = pl.BlockSpec((GROUPS, TOKEN_TILE, LANES), lambda i: (0, i, 0))
    return pl.pallas_call(
        _out_mlp_kernel,
        grid=(n_tok // TOKEN_TILE,),
        in_specs=[
            pl.BlockSpec((TOKEN_TILE, D_MODEL), lambda i: (i, 0)),
            mix, mix,
            const((MIX_WIDTH, D_MODEL)),
            const((1, D_MODEL)),
            const((D_MODEL, D_FF)),
            const((D_FF, D_MODEL)),
            const((1, D_MODEL)),
        ],
        out_specs=pl.BlockSpec((TOKEN_TILE, D_MODEL), lambda i: (i, 0)),
        out_shape=jax.ShapeDtypeStruct((n_tok, D_MODEL), _F32),
        scratch_shapes=[pltpu.VMEM((TOKEN_TILE, D_MODEL), _BF16),
                        pltpu.VMEM((TOKEN_TILE, D_MODEL), _F32)],
        compiler_params=pltpu.CompilerParams(vmem_limit_bytes=VMEM_LIMIT_BYTES),
        name="out_mlp",
    )(x2d, mix_d, mix_s, w_out, g2, w_up, w_dn, g3)


def _query_col_scale():
    qs = ATTN_SCALE * LOG2E
    one = jnp.ones((DIFF_WIDTH,), _F32)
    return jnp.concatenate([qs * one, one, one, qs * one, one, one]).reshape(1, 3 * MIX_WIDTH)


def kernel(x, attn_norm, w_in, lambda_q1, lambda_k1, lambda_q2, lambda_k2, diff_subln, sb_subln,
           w_out, mlp_norm, w_up, w_down, final_norm):
    assert attn_norm.shape[0] == 1, "single-layer block"
    b, s, d = x.shape
    x2d = x.reshape(b * s, d)

    qkv_g = _in_proj(x2d, attn_norm, w_in[0].astype(_BF16), _query_col_scale())

    slopes2 = jnp.asarray([LOG2E * 2.0 ** (-8.0 * (h + 1) / N_DIFF_HEADS)
                           for h in range(N_DIFF_HEADS)], dtype=_F32)
    mix_d = _diff_attn(qkv_g, b, slopes2, lambda_q1, lambda_k1, lambda_q2, lambda_k2, diff_subln)
    mix_s = _sb_attn(qkv_g, b, jnp.tile(sb_subln, (1, 2)))

    out = _out_mlp(x2d, mix_d, mix_s, w_out[0].astype(_BF16), mlp_norm,
                   w_up[0].astype(_BF16), w_down[0].astype(_BF16), final_norm.reshape(1, d))
    return out.reshape(b, s, d)
```

```python
import math

import jax
import jax.numpy as jnp
import numpy as np
from jax import lax
from jax.experimental import pallas as pl
from jax.experimental.pallas import tpu as pltpu

D_MODEL = 1024
HEAD_DIM = 64
DIFF_WIDTH = D_MODEL // 2
N_DIFF_HEADS = DIFF_WIDTH // (2 * HEAD_DIM)
SB_WIDTH = D_MODEL - DIFF_WIDTH
N_SB_HEADS = SB_WIDTH // HEAD_DIM
N_SB_PAIRS = N_SB_HEADS // 2
MIX_WIDTH = DIFF_WIDTH + SB_WIDTH
D_FF = 4 * D_MODEL
EPS = 1e-6
LAYER_IDX = 0
LAM_INIT = 0.8 - 0.6 * math.exp(-0.3 * LAYER_IDX)
ATTN_SCALE = 1.0 / math.sqrt(HEAD_DIM)
LOG2E = math.log2(math.e)

LANES = 128
GROUPS = 4
ATTN_TILE = 256
TOKEN_TILE = 512
FF_CHUNK = 1024
MASKED = -1e30
VMEM_LIMIT_BYTES = 56 * 1024 * 1024

_F32 = jnp.float32
_BF16 = jnp.bfloat16


def _rmsnorm(x, gain):
    inv = lax.rsqrt(jnp.mean(x * x, axis=-1, keepdims=True) + EPS)
    return x * inv * gain


def _dot(a, b):
    return jnp.dot(a, b, preferred_element_type=_F32)


def _dot_nt(a, b):
    return lax.dot_general(a, b, (((1,), (1,)), ((), ())), preferred_element_type=_F32)


def _lane_tile(x, reps):
    return jnp.concatenate([x] * reps, axis=1)


def _in_proj_kernel(x_ref, g_ref, w_ref, cs_ref, o_ref):
    a = _rmsnorm(x_ref[...], g_ref[...])
    res = (_dot(a.astype(_BF16), w_ref[...]) * cs_ref[...]).astype(_BF16)
    for g in range(o_ref.shape[0]):
        o_ref[g] = res[:, g * LANES:(g + 1) * LANES]


def _in_proj(x2d, gain, w_bf16, col_scale):
    n_tok = x2d.shape[0]
    n_out = w_bf16.shape[1]
    n_groups = n_out // LANES
    return pl.pallas_call(
        _in_proj_kernel,
        grid=(n_tok // TOKEN_TILE,),
        in_specs=[
            pl.BlockSpec((TOKEN_TILE, D_MODEL), lambda i: (i, 0)),
            pl.BlockSpec((1, D_MODEL), lambda i: (0, 0)),
            pl.BlockSpec((D_MODEL, n_out), lambda i: (0, 0)),
            pl.BlockSpec((1, n_out), lambda i: (0, 0)),
        ],
        out_specs=pl.BlockSpec((n_groups, TOKEN_TILE, LANES), lambda i: (0, i, 0)),
        out_shape=jax.ShapeDtypeStruct((n_groups, n_tok, LANES), _BF16),
        compiler_params=pltpu.CompilerParams(vmem_limit_bytes=VMEM_LIMIT_BYTES),
        name="in_proj",
    )(x2d, gain, w_bf16, col_scale)


def _split_halves(x_bf16):
    lane = lax.broadcasted_iota(jnp.int32, (1, LANES), 1)
    zero = jnp.zeros_like(x_bf16)
    return (jnp.where(lane < HEAD_DIM, x_bf16, zero), jnp.where(lane >= HEAD_DIM, x_bf16, zero))


def _rows(blk):
    start = blk * ATTN_TILE
    if not isinstance(start, int):
        start = pl.multiple_of(start, ATTN_TILE)
    return pl.ds(start, ATTN_TILE)


def _as_f32(i):
    return jnp.asarray(i, jnp.int32).astype(_F32)


def _group_spec(seq, section):
    return pl.BlockSpec((GROUPS, seq, LANES), lambda bi: (section, bi, 0))


def _offdiag_steps(n_blk):
    steps = [(qi, kj) for qi in range(1, n_blk) for kj in range(qi - 1, -1, -1)]
    return (jnp.asarray(np.array([qk[0] for qk in steps], np.int32)),
            jnp.asarray(np.array([qk[1] for qk in steps], np.int32)))


def _skewed_pipeline(n_units, stages):
    depth = len(stages)
    assert n_units >= depth

    def iteration(j, parity, lo, hi):
        for k in range(depth - 1, -1, -1):
            if lo <= k < hi:
                stages[k](j - k, (parity - k) % 2)

    for j in range(depth - 1):
        iteration(j, j % 2, 0, j + 1)

    start = depth - 1
    n_pairs, odd = divmod(n_units - start, 2)

    def body(i, c):
        j = start + 2 * i
        iteration(j, start % 2, 0, depth)
        iteration(j + 1, (start + 1) % 2, 0, depth)
        return c

    lax.fori_loop(0, n_pairs, body, 0)
    if odd:
        iteration(n_units - 1, (n_units - 1) % 2, 0, depth)
    for j in range(n_units, n_units + depth - 1):
        iteration(j, j % 2, j - n_units + 1, depth)


def _diff_attn_kernel(steps_q_ref, steps_k_ref, slopes_ref, lq1_ref, lk1_ref, lq2_ref, lk2_ref,
                      g_ref, q_ref, k_ref, v_ref, o_ref,
                      m_ref, a_ref, s_ref, p_ref, al_ref, dbias_ref):
    t = ATTN_TILE
    reps = t // LANES
    n_blk = q_ref.shape[1] // t
    n_h = q_ref.shape[0]
    h_shift = n_h.bit_length() - 1
    assert n_h == 1 << h_shift

    lam = (jnp.exp(jnp.sum(lq1_ref[...] * lk1_ref[...], axis=-1, keepdims=True))
           - jnp.exp(jnp.sum(lq2_ref[...] * lk2_ref[...], axis=-1, keepdims=True))
           + LAM_INIT)

    assert n_blk % 2 == 0
    wide = 2 * t
    row = lax.broadcasted_iota(jnp.int32, (t, t), 0)
    col = lax.broadcasted_iota(jnp.int32, (t, t), 1)
    colf = lax.broadcasted_iota(jnp.int32, (1, wide), 1).astype(_F32)
    for h in range(n_h):
        diag = jnp.where(row >= col, slopes_ref[h] * col.astype(_F32), MASKED)
        dbias_ref[h, :, t:] = diag
        dbias_ref[h, :, :t] = jnp.broadcast_to(slopes_ref[h] * (colf[:, :t] - t), (t, t))

    def make_stages(kind):
        first = kind != "full"
        width = t if kind == "diag" else wide
        ones = jnp.ones((width, LANES), _BF16)

        def decode(u):
            step = u >> h_shift
            head = u & (n_h - 1)
            if kind == "diag":
                return 2 * step, 2 * step * t, head
            if kind == "pair_diag":
                return 2 * step + 1, 2 * step * t, head
            return steps_q_ref[step], steps_k_ref[step] * wide, head

        def key_rows(start):
            if not isinstance(start, int):
                start = pl.multiple_of(start, t)
            return pl.ds(start, width)

        def scores(u, slot):
            qi, k0, head = decode(u)
            q_halves = _split_halves(q_ref[head, _rows(qi), :])
            k = k_ref[head, key_rows(k0), :]
            if kind == "diag":
                bias = dbias_ref[head, :, t:]
            elif kind == "pair_diag":
                bias = dbias_ref[head]
            else:
                bias = slopes_ref[head] * (colf + _as_f32(k0 - qi * t))
            for half in range(2):
                s_ref[slot, half, :, :width] = _dot_nt(q_halves[half], k) + bias

        def probabilities(u, slot):
            qi, k0, head = decode(u)
            for half in range(2):
                idx = 2 * head + half
                s = s_ref[slot, half, :, :width]
                m_curr = jnp.max(s, axis=1, keepdims=True)
                if first:
                    m_new = jnp.broadcast_to(m_curr, (t, LANES))
                else:
                    m_prev = m_ref[qi, idx]
                    m_new = jnp.maximum(m_prev, m_curr)
                    al_ref[slot, half] = jnp.exp2(m_prev - m_new)
                p = jnp.exp2(s - _lane_tile(m_new, width // LANES))
                p_ref[slot, half, :, :width] = p.astype(_BF16)
                m_ref[qi, idx] = m_new

        def values(u, slot):
            qi, k0, head = decode(u)
            v_aug = jnp.concatenate([v_ref[head, key_rows(k0), :], ones], axis=1)
            for half in range(2):
                idx = 2 * head + half
                pv = _dot(p_ref[slot, half, :, :width], v_aug)
                if first:
                    a_ref[qi, idx] = pv
                else:
                    a_ref[qi, idx] = _lane_tile(al_ref[slot, half], 2) * a_ref[qi, idx] + pv

        return scores, probabilities, values

    _skewed_pipeline(n_blk // 2 * n_h, make_stages("diag"))
    _skewed_pipeline(n_blk // 2 * n_h, make_stages("pair_diag"))
    _skewed_pipeline(steps_q_ref.shape[0] * n_h, make_stages("full"))

    def finish(qi, carry):
        for h in range(n_h):
            a1 = a_ref[qi, 2 * h]
            a2 = a_ref[qi, 2 * h + 1]
            o = a1[:, :LANES] / a1[:, LANES:] - lam * (a2[:, :LANES] / a2[:, LANES:])
            o = _rmsnorm(o, g_ref[...]) * (1.0 - LAM_INIT)
            o_ref[h, _rows(qi), :] = o.astype(o_ref.dtype)
        return carry

    lax.fori_loop(0, n_blk, finish, 0)


def _diff_attn(qkv_g, batch, slopes2, lq1, lk1, lq2, lk2, gain):
    n_tok = qkv_g.shape[1]
    s = n_tok // batch
    n_h = N_DIFF_HEADS
    n_blk = s // ATTN_TILE
    steps = [(qi, c) for qi in range(2, n_blk) for c in range(qi // 2)]
    steps_q = jnp.asarray(np.array([qc[0] for qc in steps], np.int32))
    steps_k = jnp.asarray(np.array([qc[1] for qc in steps], np.int32))
    smem = pl.BlockSpec(memory_space=pltpu.SMEM)
    vec = lambda n: pl.BlockSpec((1, n), lambda bi: (0, 0))
    handoff = lambda width, dtype: pltpu.VMEM((2, 2, ATTN_TILE, width), dtype)
    return pl.pallas_call(
        _diff_attn_kernel,
        grid=(batch,),
        in_specs=[
            smem, smem, smem,
            vec(HEAD_DIM), vec(HEAD_DIM), vec(HEAD_DIM), vec(HEAD_DIM), vec(LANES),
            _group_spec(s, 0), _group_spec(s, 1), _group_spec(s, 2),
        ],
        out_specs=pl.BlockSpec((GROUPS, s, LANES), lambda bi: (0, bi, 0)),
        out_shape=jax.ShapeDtypeStruct((GROUPS, n_tok, LANES), _BF16),
        scratch_shapes=[pltpu.VMEM((n_blk, 2 * n_h, ATTN_TILE, LANES), _F32),
                        pltpu.VMEM((n_blk, 2 * n_h, ATTN_TILE, 2 * LANES), _F32),
                        handoff(2 * ATTN_TILE, _F32), handoff(2 * ATTN_TILE, _BF16),
                        handoff(LANES, _F32),
                        pltpu.VMEM((n_h, ATTN_TILE, 2 * ATTN_TILE), _F32)],
        compiler_params=pltpu.CompilerParams(vmem_limit_bytes=VMEM_LIMIT_BYTES),
        name="diff_attn",
    )(steps_q, steps_k, slopes2, lq1, lk1, lq2, lk2, gain, qkv_g, qkv_g, qkv_g)


def _sb_attn_kernel(steps_q_ref, steps_k_ref, g_ref, q_ref, k_ref, v_ref, o_ref,
                    c_ref, acc_ref, z_ref, x_ref, lb_ref, w_ref, tri_ref, dmask_ref):
    t = ATTN_TILE
    reps = t // LANES
    n_blk = q_ref.shape[1] // t
    n_g = q_ref.shape[0]
    g_shift = n_g.bit_length() - 1
    assert n_g == 1 << g_shift

    row = lax.broadcasted_iota(jnp.int32, (t, t), 0)
    col = lax.broadcasted_iota(jnp.int32, (t, t), 1)
    tri_ref[...] = jnp.where(row > col, -1.0, 0.0).astype(_BF16)
    dmask_ref[...] = jnp.where(col < row, 0.0, MASKED)
    lane = lax.broadcasted_iota(jnp.int32, (1, LANES), 1)
    lo, hi = lane < HEAD_DIM, lane >= HEAD_DIM

    def make_stages(diagonal):
        def decode(u):
            step = u >> g_shift
            grp = u & (n_g - 1)
            if diagonal:
                return step, step, grp
            return steps_q_ref[step], steps_k_ref[step], grp

        def logits(u, slot):
            qi, kj, grp = decode(u)
            q_halves = _split_halves(q_ref[grp, _rows(qi), :])
            k = k_ref[grp, _rows(kj), :]
            for half in range(2):
                z = _dot_nt(q_halves[half], k)
                if diagonal:
                    z = z + dmask_ref[...]
                z_ref[slot, half] = z

        def log_terms(u, slot):
            qi, kj, grp = decode(u)
            for half in range(2):
                head = 2 * grp + half
                z = z_ref[slot, half]
                sp = jnp.log(1.0 + jnp.exp2(-jnp.abs(z))) * LOG2E
                neg_log_om = jnp.maximum(z, 0.0) + sp
                log_beta = z - neg_log_om
                om_sum = jnp.sum(neg_log_om, axis=1, keepdims=True)
                x_ref[slot, half] = neg_log_om.astype(_BF16)
                if diagonal:
                    lb_ref[slot, half] = log_beta
                    c_ref[qi, head] = jnp.broadcast_to(om_sum, (t, LANES))
                else:
                    carry = c_ref[qi, head]
                    lb_ref[slot, half] = log_beta - _lane_tile(carry, reps)
                    c_ref[qi, head] = carry + om_sum

        def weights(u, slot):
            for half in range(2):
                later = _dot(x_ref[slot, half], tri_ref[...])
                w_ref[slot, half] = jnp.exp2(later + lb_ref[slot, half]).astype(_BF16)

        def values(u, slot):
            qi, kj, grp = decode(u)
            v_halves = _split_halves(v_ref[grp, _rows(kj), :])
            pv = (_dot(w_ref[slot, 0], v_halves[0])
                  + _dot(w_ref[slot, 1], v_halves[1]))
            if diagonal:
                acc_ref[qi, grp] = pv
            else:
                acc_ref[qi, grp] += pv

        return logits, log_terms, weights, values

    _skewed_pipeline(n_blk * n_g, make_stages(True))
    _skewed_pipeline(steps_q_ref.shape[0] * n_g, make_stages(False))

    def finish(qi, carry):
        for grp in range(n_g):
            o = acc_ref[qi, grp]
            sq = o * o
            ms_lo = jnp.sum(jnp.where(lo, sq, 0.0), axis=-1, keepdims=True) * (1.0 / HEAD_DIM)
            ms_hi = jnp.sum(jnp.where(hi, sq, 0.0), axis=-1, keepdims=True) * (1.0 / HEAD_DIM)
            inv = jnp.where(lo, lax.rsqrt(ms_lo + EPS), lax.rsqrt(ms_hi + EPS))
            o_ref[grp, _rows(qi), :] = (o * inv * g_ref[...]).astype(o_ref.dtype)
        return carry

    lax.fori_loop(0, n_blk, finish, 0)


def _sb_attn(qkv_g, batch, gain2):
    n_tok = qkv_g.shape[1]
    s = n_tok // batch
    n_blk = s // ATTN_TILE
    steps_q, steps_k = _offdiag_steps(n_blk)
    smem = pl.BlockSpec(memory_space=pltpu.SMEM)
    return pl.pallas_call(
        _sb_attn_kernel,
        grid=(batch,),
        in_specs=[
            smem, smem,
            pl.BlockSpec((1, LANES), lambda bi: (0, 0)),
            _group_spec(s, 3), _group_spec(s, 4), _group_spec(s, 5),
        ],
        out_specs=pl.BlockSpec((GROUPS, s, LANES), lambda bi: (0, bi, 0)),
        out_shape=jax.ShapeDtypeStruct((GROUPS, n_tok, LANES), _BF16),
        scratch_shapes=[pltpu.VMEM((n_blk, N_SB_HEADS, ATTN_TILE, LANES), _F32),
                        pltpu.VMEM((n_blk, GROUPS, ATTN_TILE, LANES), _F32),
                        pltpu.VMEM((2, 2, ATTN_TILE, ATTN_TILE), _F32),
                        pltpu.VMEM((2, 2, ATTN_TILE, ATTN_TILE), _BF16),
                        pltpu.VMEM((2, 2, ATTN_TILE, ATTN_TILE), _F32),
                        pltpu.VMEM((2, 2, ATTN_TILE, ATTN_TILE), _BF16),
                        pltpu.VMEM((ATTN_TILE, ATTN_TILE), _BF16),
                        pltpu.VMEM((ATTN_TILE, ATTN_TILE), _F32)],
        compiler_params=pltpu.CompilerParams(vmem_limit_bytes=VMEM_LIMIT_BYTES),
        name="sb_attn",
    )(steps_q, steps_k, gain2, qkv_g, qkv_g, qkv_g)


def _out_mlp_kernel(x_ref, md_ref, ms_ref, wo_ref, g2_ref, wup_ref, wdn_ref,
                    g3_ref, o_ref, m_ref, acc_ref):
    mixed = jnp.concatenate([md_ref[g] for g in range(md_ref.shape[0])]
                            + [ms_ref[g] for g in range(ms_ref.shape[0])], axis=1)
    h = x_ref[...] + _dot(mixed, wo_ref[...])
    m_ref[...] = _rmsnorm(h, g2_ref[...]).astype(_BF16)
    acc_ref[...] = h

    def ff_chunk(c, carry):
        cols = pl.ds(pl.multiple_of(c * FF_CHUNK, FF_CHUNK), FF_CHUNK)
        u = jnp.square(jnp.maximum(_dot(m_ref[...], wup_ref[:, cols]), 0.0))
        acc_ref[...] += _dot(u.astype(_BF16), wdn_ref[cols, :])
        return carry

    lax.fori_loop(0, D_FF // FF_CHUNK, ff_chunk, 0)
    o_ref[...] = _rmsnorm(acc_ref[...], g3_ref[...])


def _out_mlp(x2d, mix_d, mix_s, w_out, g2, w_up, w_dn, g3):
    n_tok = x2d.shape[0]
    const = lambda shape: pl.BlockSpec(shape, lambda i: (0, 0), pipeline_mode=pl.Buffered(1))
    mix = pl.BlockSpec((GROUPS, TOKEN_TILE, LANES), lambda i: (0, i, 0))
    return pl.pallas_call(
        _out_mlp_kernel,
        grid=(n_tok // TOKEN_TILE,),
        in_specs=[
            pl.BlockSpec((TOKEN_TILE, D_MODEL), lambda i: (i, 0)),
            mix, mix,
            const((MIX_WIDTH, D_MODEL)),
            const((1, D_MODEL)),
            const((D_MODEL, D_FF)),
            const((D_FF, D_MODEL)),
            const((1, D_MODEL)),
        ],
        out_specs=pl.BlockSpec((TOKEN_TILE, D_MODEL), lambda i: (i, 0)),
        out_shape=jax.ShapeDtypeStruct((n_tok, D_MODEL), _F32),
        scratch_shapes=[pltpu.VMEM((TOKEN_TILE, D_MODEL), _BF16),
                        pltpu.VMEM((TOKEN_TILE, D_MODEL), _F32)],
        compiler_params=pltpu.CompilerParams(vmem_limit_bytes=VMEM_LIMIT_BYTES),
        name="out_mlp",
    )(x2d, mix_d, mix_s, w_out, g2, w_up, w_dn, g3)


def _query_col_scale():
    qs = ATTN_SCALE * LOG2E
    one = jnp.ones((DIFF_WIDTH,), _F32)
    return jnp.concatenate([qs * one, one, one, qs * one, one, one]).reshape(1, 3 * MIX_WIDTH)


def kernel(x, attn_norm, w_in, lambda_q1, lambda_k1, lambda_q2, lambda_k2, diff_subln, sb_subln,
           w_out, mlp_norm, w_up, w_down, final_norm):
    assert attn_norm.shape[0] == 1, "single-layer block"
    b, s, d = x.shape
    x2d = x.reshape(b * s, d)

    qkv_g = _in_proj(x2d, attn_norm, w_in[0].astype(_BF16), _query_col_scale())

    slopes2 = jnp.asarray([LOG2E * 2.0 ** (-8.0 * (h + 1) / N_DIFF_HEADS)
                           for h in range(N_DIFF_HEADS)], dtype=_F32)
    mix_d = _diff_attn(qkv_g, b, slopes2, lambda_q1, lambda_k1, lambda_q2, lambda_k2, diff_subln)
    mix_s = _sb_attn(qkv_g, b, jnp.tile(sb_subln, (1, 2)))

    out = _out_mlp(x2d, mix_d, mix_s, w_out[0].astype(_BF16), mlp_norm,
                   w_up[0].astype(_BF16), w_down[0].astype(_BF16), final_norm.reshape(1, d))
    return out.reshape(b, s, d)
```

```python
import math

import jax
import jax.numpy as jnp
import numpy as np
from jax import lax
from jax.experimental import pallas as pl
from jax.experimental.pallas import tpu as pltpu

D_MODEL = 1024
HEAD_DIM = 64
DIFF_WIDTH = D_MODEL // 2
N_DIFF_HEADS = DIFF_WIDTH // (2 * HEAD_DIM)
SB_WIDTH = D_MODEL - DIFF_WIDTH
N_SB_HEADS = SB_WIDTH // HEAD_DIM
N_SB_PAIRS = N_SB_HEADS // 2
MIX_WIDTH = DIFF_WIDTH + SB_WIDTH
D_FF = 4 * D_MODEL
EPS = 1e-6
LAYER_IDX = 0
LAM_INIT = 0.8 - 0.6 * math.exp(-0.3 * LAYER_IDX)
ATTN_SCALE = 1.0 / math.sqrt(HEAD_DIM)
LOG2E = math.log2(math.e)

LANES = 128
GROUPS = 4
ATTN_TILE = 256
TOKEN_TILE = 512
FF_CHUNK = 1024
MASKED = -1e30
VMEM_LIMIT_BYTES = 56 * 1024 * 1024

_F32 = jnp.float32
_BF16 = jnp.bfloat16


def _rmsnorm(x, gain):
    inv = lax.rsqrt(jnp.mean(x * x, axis=-1, keepdims=True) + EPS)
    return x * inv * gain


def _dot(a, b):
    return jnp.dot(a, b, preferred_element_type=_F32)


def _dot_nt(a, b):
    return lax.dot_general(a, b, (((1,), (1,)), ((), ())), preferred_element_type=_F32)


def _lane_tile(x, reps):
    return jnp.concatenate([x] * reps, axis=1)


def _in_proj_kernel(x_ref, g_ref, w_ref, cs_ref, o_ref):
    a = _rmsnorm(x_ref[...], g_ref[...])
    res = (_dot(a.astype(_BF16), w_ref[...]) * cs_ref[...]).astype(_BF16)
    for g in range(o_ref.shape[0]):
        o_ref[g] = res[:, g * LANES:(g + 1) * LANES]


def _in_proj(x2d, gain, w_bf16, col_scale):
    n_tok = x2d.shape[0]
    n_out = w_bf16.shape[1]
    n_groups = n_out // LANES
    return pl.pallas_call(
        _in_proj_kernel,
        grid=(n_tok // TOKEN_TILE,),
        in_specs=[
            pl.BlockSpec((TOKEN_TILE, D_MODEL), lambda i: (i, 0)),
            pl.BlockSpec((1, D_MODEL), lambda i: (0, 0)),
            pl.BlockSpec((D_MODEL, n_out), lambda i: (0, 0)),
            pl.BlockSpec((1, n_out), lambda i: (0, 0)),
        ],
        out_specs=pl.BlockSpec((n_groups, TOKEN_TILE, LANES), lambda i: (0, i, 0)),
        out_shape=jax.ShapeDtypeStruct((n_groups, n_tok, LANES), _BF16),
        compiler_params=pltpu.CompilerParams(vmem_limit_bytes=VMEM_LIMIT_BYTES),
        name="in_proj",
    )(x2d, gain, w_bf16, col_scale)


def _split_halves(x_bf16):
    lane = lax.broadcasted_iota(jnp.int32, (1, LANES), 1)
    zero = jnp.zeros_like(x_bf16)
    return (jnp.where(lane < HEAD_DIM, x_bf16, zero), jnp.where(lane >= HEAD_DIM, x_bf16, zero))


def _rows(blk):
    start = blk * ATTN_TILE
    if not isinstance(start, int):
        start = pl.multiple_of(start, ATTN_TILE)
    return pl.ds(start, ATTN_TILE)


def _as_f32(i):
    return jnp.asarray(i, jnp.int32).astype(_F32)


def _group_spec(seq, section):
    return pl.BlockSpec((GROUPS, seq, LANES), lambda bi: (section, bi, 0))


def _offdiag_steps(n_blk):
    steps = [(qi, kj) for qi in range(1, n_blk) for kj in range(qi - 1, -1, -1)]
    return (jnp.asarray(np.array([qk[0] for qk in steps], np.int32)),
            jnp.asarray(np.array([qk[1] for qk in steps], np.int32)))


def _skewed_pipeline(n_units, stages):
    depth = len(stages)
    assert n_units >= depth

    def iteration(j, parity, lo, hi):
        for k in range(depth - 1, -1, -1):
            if lo <= k < hi:
                stages[k](j - k, (parity - k) % 2)

    for j in range(depth - 1):
        iteration(j, j % 2, 0, j + 1)

    start = depth - 1
    n_pairs, odd = divmod(n_units - start, 2)

    def body(i, c):
        j = start + 2 * i
        iteration(j, start % 2, 0, depth)
        iteration(j + 1, (start + 1) % 2, 0, depth)
        return c

    lax.fori_loop(0, n_pairs, body, 0)
    if odd:
        iteration(n_units - 1, (n_units - 1) % 2, 0, depth)
    for j in range(n_units, n_units + depth - 1):
        iteration(j, j % 2, j - n_units + 1, depth)


def _diff_attn_kernel(steps_q_ref, steps_k_ref, slopes_ref, lq1_ref, lk1_ref, lq2_ref, lk2_ref,
                      g_ref, q_ref, k_ref, v_ref, o_ref,
                      m_ref, a_ref, s_ref, p_ref, al_ref, dbias_ref):
    t = ATTN_TILE
    reps = t // LANES
    n_blk = q_ref.shape[1] // t
    n_h = q_ref.shape[0]
    h_shift = n_h.bit_length() - 1
    assert n_h == 1 << h_shift

    lam = (jnp.exp(jnp.sum(lq1_ref[...] * lk1_ref[...], axis=-1, keepdims=True))
           - jnp.exp(jnp.sum(lq2_ref[...] * lk2_ref[...], axis=-1, keepdims=True))
           + LAM_INIT)

    assert n_blk % 2 == 0
    wide = 2 * t
    row = lax.broadcasted_iota(jnp.int32, (t, t), 0)
    col = lax.broadcasted_iota(jnp.int32, (t, t), 1)
    colf = lax.broadcasted_iota(jnp.int32, (1, wide), 1).astype(_F32)
    for h in range(n_h):
        diag = jnp.where(row >= col, slopes_ref[h] * col.astype(_F32), MASKED)
        dbias_ref[h, :, t:] = diag
        dbias_ref[h, :, :t] = jnp.broadcast_to(slopes_ref[h] * (colf[:, :t] - t), (t, t))

    def make_stages(kind):
        first = kind != "full"
        width = t if kind == "diag" else wide
        ones = jnp.ones((width, LANES), _BF16)

        def decode(u):
            step = u >> h_shift
            head = u & (n_h - 1)
            if kind == "diag":
                return 2 * step, 2 * step * t, head
            if kind == "pair_diag":
                return 2 * step + 1, 2 * step * t, head
            return steps_q_ref[step], steps_k_ref[step] * wide, head

        def key_rows(start):
            if not isinstance(start, int):
                start = pl.multiple_of(start, t)
            return pl.ds(start, width)

        def scores(u, slot):
            qi, k0, head = decode(u)
            q_halves = _split_halves(q_ref[head, _rows(qi), :])
            k = k_ref[head, key_rows(k0), :]
            if kind == "diag":
                bias = dbias_ref[head, :, t:]
            elif kind == "pair_diag":
                bias = dbias_ref[head]
            else:
                bias = slopes_ref[head] * (colf + _as_f32(k0 - qi * t))
            for half in range(2):
                s_ref[slot, half, :, :width] = _dot_nt(q_halves[half], k) + bias

        def probabilities(u, slot):
            qi, k0, head = decode(u)
            for half in range(2):
                idx = 2 * head + half
                s = s_ref[slot, half, :, :width]
                m_curr = jnp.max(s, axis=1, keepdims=True)
                if first:
                    m_new = jnp.broadcast_to(m_curr, (t, LANES))
                else:
                    m_prev = m_ref[qi, idx]
                    m_new = jnp.maximum(m_prev, m_curr)
                    al_ref[slot, half] = jnp.exp2(m_prev - m_new)
                p = jnp.exp2(s - _lane_tile(m_new, width // LANES))
                p_ref[slot, half, :, :width] = p.astype(_BF16)
                m_ref[qi, idx] = m_new

        def values(u, slot):
            qi, k0, head = decode(u)
            v_aug = jnp.concatenate([v_ref[head, key_rows(k0), :], ones], axis=1)
            for half in range(2):
                idx = 2 * head + half
                pv = _dot(p_ref[slot, half, :, :width], v_aug)
                if first:
                    a_ref[qi, idx] = pv
                else:
                    a_ref[qi, idx] = _lane_tile(al_ref[slot, half], 2) * a_ref[qi, idx] + pv

        return scores, probabilities, values

    _skewed_pipeline(n_blk // 2 * n_h, make_stages("diag"))
    _skewed_pipeline(n_blk // 2 * n_h, make_stages("pair_diag"))
    _skewed_pipeline(steps_q_ref.shape[0] * n_h, make_stages("full"))

    def finish(qi, carry):
        for h in range(n_h):
            a1 = a_ref[qi, 2 * h]
            a2 = a_ref[qi, 2 * h + 1]
            o = a1[:, :LANES] / a1[:, LANES:] - lam * (a2[:, :LANES] / a2[:, LANES:])
            o = _rmsnorm(o, g_ref[...]) * (1.0 - LAM_INIT)
            o_ref[h, _rows(qi), :] = o.astype(o_ref.dtype)
        return carry

    lax.fori_loop(0, n_blk, finish, 0)


def _diff_attn(qkv_g, batch, slopes2, lq1, lk1, lq2, lk2, gain):
    n_tok = qkv_g.shape[1]
    s = n_tok // batch
    n_h = N_DIFF_HEADS
    n_blk = s // ATTN_TILE
    steps = [(qi, c) for qi in range(2, n_blk) for c in range(qi // 2)]
    steps_q = jnp.asarray(np.array([qc[0] for qc in steps], np.int32))
    steps_k = jnp.asarray(np.array([qc[1] for qc in steps], np.int32))
    smem = pl.BlockSpec(memory_space=pltpu.SMEM)
    vec = lambda n: pl.BlockSpec((1, n), lambda bi: (0, 0))
    handoff = lambda width, dtype: pltpu.VMEM((2, 2, ATTN_TILE, width), dtype)
    return pl.pallas_call(
        _diff_attn_kernel,
        grid=(batch,),
        in_specs=[
            smem, smem, smem,
            vec(HEAD_DIM), vec(HEAD_DIM), vec(HEAD_DIM), vec(HEAD_DIM), vec(LANES),
            _group_spec(s, 0), _group_spec(s, 1), _group_spec(s, 2),
        ],
        out_specs=pl.BlockSpec((GROUPS, s, LANES), lambda bi: (0, bi, 0)),
        out_shape=jax.ShapeDtypeStruct((GROUPS, n_tok, LANES), _BF16),
        scratch_shapes=[pltpu.VMEM((n_blk, 2 * n_h, ATTN_TILE, LANES), _F32),
                        pltpu.VMEM((n_blk, 2 * n_h, ATTN_TILE, 2 * LANES), _F32),
                        handoff(2 * ATTN_TILE, _F32), handoff(2 * ATTN_TILE, _BF16),
                        handoff(LANES, _F32),
                        pltpu.VMEM((n_h, ATTN_TILE, 2 * ATTN_TILE), _F32)],
        compiler_params=pltpu.CompilerParams(vmem_limit_bytes=VMEM_LIMIT_BYTES),
        name="diff_attn",
    )(steps_q, steps_k, slopes2, lq1, lk1, lq2, lk2, gain, qkv_g, qkv_g, qkv_g)


def _sb_attn_kernel(steps_q_ref, steps_k_ref, g_ref, q_ref, k_ref, v_ref, o_ref,
                    c_ref, acc_ref, z_ref, x_ref, lb_ref, w_ref, tri_ref, dmask_ref,
                    qm_ref, vm_ref):
    t = ATTN_TILE
    reps = t // LANES
    n_blk = q_ref.shape[1] // t
    n_g = q_ref.shape[0]
    g_shift = n_g.bit_length() - 1
    assert n_g == 1 << g_shift

    row = lax.broadcasted_iota(jnp.int32, (t, t), 0)
    col = lax.broadcasted_iota(jnp.int32, (t, t), 1)
    tri_ref[...] = jnp.where(row > col, -1.0, 0.0).astype(_BF16)
    dmask_ref[...] = jnp.where(col < row, 0.0, MASKED)
    lane = lax.broadcasted_iota(jnp.int32, (1, LANES), 1)
    lo, hi = lane < HEAD_DIM, lane >= HEAD_DIM

    def split_block(j, c):
        for grp in range(n_g):
            for src_ref, dst_ref in ((q_ref, qm_ref), (v_ref, vm_ref)):
                halves = _split_halves(src_ref[grp, _rows(j), :])
                dst_ref[grp, 0, _rows(j), :] = halves[0]
                dst_ref[grp, 1, _rows(j), :] = halves[1]
        return c

    lax.fori_loop(0, n_blk, split_block, 0)

    def make_stages(diagonal):
        def decode(u):
            step = u >> g_shift
            grp = u & (n_g - 1)
            if diagonal:
                return step, step, grp
            return steps_q_ref[step], steps_k_ref[step], grp

        def logits(u, slot):
            qi, kj, grp = decode(u)
            k = k_ref[grp, _rows(kj), :]
            for half in range(2):
                z = _dot_nt(qm_ref[grp, half, _rows(qi), :], k)
                if diagonal:
                    z = z + dmask_ref[...]
                z_ref[slot, half] = z

        def log_terms(u, slot):
            qi, kj, grp = decode(u)
            for half in range(2):
                head = 2 * grp + half
                z = z_ref[slot, half]
                sp = jnp.log(1.0 + jnp.exp2(-jnp.abs(z))) * LOG2E
                neg_log_om = jnp.maximum(z, 0.0) + sp
                log_beta = z - neg_log_om
                om_sum = jnp.sum(neg_log_om, axis=1, keepdims=True)
                x_ref[slot, half] = neg_log_om.astype(_BF16)
                if diagonal:
                    lb_ref[slot, half] = log_beta
                    c_ref[qi, head] = jnp.broadcast_to(om_sum, (t, LANES))
                else:
                    carry = c_ref[qi, head]
                    lb_ref[slot, half] = log_beta - _lane_tile(carry, reps)
                    c_ref[qi, head] = carry + om_sum

        def weights(u, slot):
            for half in range(2):
                later = _dot(x_ref[slot, half], tri_ref[...])
                w_ref[slot, half] = jnp.exp2(later + lb_ref[slot, half]).astype(_BF16)

        def values(u, slot):
            qi, kj, grp = decode(u)
            pv = (_dot(w_ref[slot, 0], vm_ref[grp, 0, _rows(kj), :])
                  + _dot(w_ref[slot, 1], vm_ref[grp, 1, _rows(kj), :]))
            if diagonal:
                acc_ref[qi, grp] = pv
            else:
                acc_ref[qi, grp] += pv

        return logits, log_terms, weights, values

    _skewed_pipeline(n_blk * n_g, make_stages(True))
    _skewed_pipeline(steps_q_ref.shape[0] * n_g, make_stages(False))

    def finish(qi, carry):
        for grp in range(n_g):
            o = acc_ref[qi, grp]
            sq = o * o
            ms_lo = jnp.sum(jnp.where(lo, sq, 0.0), axis=-1, keepdims=True) * (1.0 / HEAD_DIM)
            ms_hi = jnp.sum(jnp.where(hi, sq, 0.0), axis=-1, keepdims=True) * (1.0 / HEAD_DIM)
            inv = jnp.where(lo, lax.rsqrt(ms_lo + EPS), lax.rsqrt(ms_hi + EPS))
            o_ref[grp, _rows(qi), :] = (o * inv * g_ref[...]).astype(o_ref.dtype)
        return carry

    lax.fori_loop(0, n_blk, finish, 0)


def _sb_attn(qkv_g, batch, gain2):
    n_tok = qkv_g.shape[1]
    s = n_tok // batch
    n_blk = s // ATTN_TILE
    steps_q, steps_k = _offdiag_steps(n_blk)
    smem = pl.BlockSpec(memory_space=pltpu.SMEM)
    return pl.pallas_call(
        _sb_attn_kernel,
        grid=(batch,),
        in_specs=[
            smem, smem,
            pl.BlockSpec((1, LANES), lambda bi: (0, 0)),
            _group_spec(s, 3), _group_spec(s, 4), _group_spec(s, 5),
        ],
        out_specs=pl.BlockSpec((GROUPS, s, LANES), lambda bi: (0, bi, 0)),
        out_shape=jax.ShapeDtypeStruct((GROUPS, n_tok, LANES), _BF16),
        scratch_shapes=[pltpu.VMEM((n_blk, N_SB_HEADS, ATTN_TILE, LANES), _F32),
                        pltpu.VMEM((n_blk, GROUPS, ATTN_TILE, LANES), _F32),
                        pltpu.VMEM((2, 2, ATTN_TILE, ATTN_TILE), _F32),
                        pltpu.VMEM((2, 2, ATTN_TILE, ATTN_TILE), _BF16),
                        pltpu.VMEM((2, 2, ATTN_TILE, ATTN_TILE), _F32),
                        pltpu.VMEM((2, 2, ATTN_TILE, ATTN_TILE), _BF16),
                        pltpu.VMEM((ATTN_TILE, ATTN_TILE), _BF16),
                        pltpu.VMEM((ATTN_TILE, ATTN_TILE), _F32),
                        pltpu.VMEM((GROUPS, 2, s, LANES), _BF16),
                        pltpu.VMEM((GROUPS, 2, s, LANES), _BF16)],
        compiler_params=pltpu.CompilerParams(vmem_limit_bytes=VMEM_LIMIT_BYTES),
        name="sb_attn",
    )(steps_q, steps_k, gain2, qkv_g, qkv_g, qkv_g)


def _out_mlp_kernel(x_ref, md_ref, ms_ref, wo_ref, g2_ref, wup_ref, wdn_ref,
                    g3_ref, o_ref, m_ref, acc_ref):
    mixed = jnp.concatenate([md_ref[g] for g in range(md_ref.shape[0])]
                            + [ms_ref[g] for g in range(ms_ref.shape[0])], axis=1)
    h = x_ref[...] + _dot(mixed, wo_ref[...])
    m_ref[...] = _rmsnorm(h, g2_ref[...]).astype(_BF16)
    acc_ref[...] = h

    def ff_chunk(c, carry):
        cols = pl.ds(pl.multiple_of(c * FF_CHUNK, FF_CHUNK), FF_CHUNK)
        u = jnp.square(jnp.maximum(_dot(m_ref[...], wup_ref[:, cols]), 0.0))
        acc_ref[...] += _dot(u.astype(_BF16), wdn_ref[cols, :])
        return carry

    lax.fori_loop(0, D_FF // FF_CHUNK, ff_chunk, 0)
    o_ref[...] = _rmsnorm(acc_ref[...], g3_ref[...])


def _out_mlp(x2d, mix_d, mix_s, w_out, g2, w_up, w_dn, g3):
    n_tok = x2d.shape[0]
    const = lambda shape: pl.BlockSpec(shape, lambda i: (0, 0), pipeline_mode=pl.Buffered(1))
    mix = pl.BlockSpec((GROUPS, TOKEN_TILE, LANES), lambda i: (0, i, 0))
    return pl.pallas_call(
        _out_mlp_kernel,
        grid=(n_tok // TOKEN_TILE,),
        in_specs=[
            pl.BlockSpec((TOKEN_TILE, D_MODEL), lambda i: (i, 0)),
            mix, mix,
            const((MIX_WIDTH, D_MODEL)),
            const((1, D_MODEL)),
            const((D_MODEL, D_FF)),
            const((D_FF, D_MODEL)),
            const((1, D_MODEL)),
        ],
        out_specs=pl.BlockSpec((TOKEN_TILE, D_MODEL), lambda i: (i, 0)),
        out_shape=jax.ShapeDtypeStruct((n_tok, D_MODEL), _F32),
        scratch_shapes=[pltpu.VMEM((TOKEN_TILE, D_MODEL), _BF16),
                        pltpu.VMEM((TOKEN_TILE, D_MODEL), _F32)],
        compiler_params=pltpu.CompilerParams(vmem_limit_bytes=VMEM_LIMIT_BYTES),
        name="out_mlp",
    )(x2d, mix_d, mix_s, w_out, g2, w_up, w_dn, g3)


def _query_col_scale():
    qs = ATTN_SCALE * LOG2E
    one = jnp.ones((DIFF_WIDTH,), _F32)
    return jnp.concatenate([qs * one, one, one, qs * one, one, one]).reshape(1, 3 * MIX_WIDTH)


def kernel(x, attn_norm, w_in, lambda_q1, lambda_k1, lambda_q2, lambda_k2, diff_subln, sb_subln,
           w_out, mlp_norm, w_up, w_down, final_norm):
    assert attn_norm.shape[0] == 1, "single-layer block"
    b, s, d = x.shape
    x2d = x.reshape(b * s, d)

    qkv_g = _in_proj(x2d, attn_norm, w_in[0].astype(_BF16), _query_col_scale())

    slopes2 = jnp.asarray([LOG2E * 2.0 ** (-8.0 * (h + 1) / N_DIFF_HEADS)
                           for h in range(N_DIFF_HEADS)], dtype=_F32)
    mix_d = _diff_attn(qkv_g, b, slopes2, lambda_q1, lambda_k1, lambda_q2, lambda_k2, diff_subln)
    mix_s = _sb_attn(qkv_g, b, jnp.tile(sb_subln, (1, 2)))

    out = _out_mlp(x2d, mix_d, mix_s, w_out[0].astype(_BF16), mlp_norm,
                   w_up[0].astype(_BF16), w_down[0].astype(_BF16), final_norm.reshape(1, d))
    return out.reshape(b, s, d)
```

```python
import math

import jax
import jax.numpy as jnp
import numpy as np
from jax import lax
from jax.experimental import pallas as pl
from jax.experimental.pallas import tpu as pltpu

D_MODEL = 1024
HEAD_DIM = 64
DIFF_WIDTH = D_MODEL // 2
N_DIFF_HEADS = DIFF_WIDTH // (2 * HEAD_DIM)
SB_WIDTH = D_MODEL - DIFF_WIDTH
N_SB_HEADS = SB_WIDTH // HEAD_DIM
N_SB_PAIRS = N_SB_HEADS // 2
MIX_WIDTH = DIFF_WIDTH + SB_WIDTH
D_FF = 4 * D_MODEL
EPS = 1e-6
LAYER_IDX = 0
LAM_INIT = 0.8 - 0.6 * math.exp(-0.3 * LAYER_IDX)
ATTN_SCALE = 1.0 / math.sqrt(HEAD_DIM)
LOG2E = math.log2(math.e)

LANES = 128
GROUPS = 4
ATTN_TILE = 256
TOKEN_TILE = 512
MLP_TOKEN_TILE = 1024
FF_CHUNK = 1024
MASKED = -1e30
VMEM_LIMIT_BYTES = 56 * 1024 * 1024

_F32 = jnp.float32
_BF16 = jnp.bfloat16


def _rmsnorm(x, gain):
    inv = lax.rsqrt(jnp.mean(x * x, axis=-1, keepdims=True) + EPS)
    return x * inv * gain


def _dot(a, b):
    return jnp.dot(a, b, preferred_element_type=_F32)


def _dot_nt(a, b):
    return lax.dot_general(a, b, (((1,), (1,)), ((), ())), preferred_element_type=_F32)


def _lane_tile(x, reps):
    return jnp.concatenate([x] * reps, axis=1)


def _in_proj_kernel(x_ref, g_ref, w_ref, cs_ref, o_ref):
    a = _rmsnorm(x_ref[...], g_ref[...])
    res = (_dot(a.astype(_BF16), w_ref[...]) * cs_ref[...]).astype(_BF16)
    for g in range(o_ref.shape[0]):
        o_ref[g] = res[:, g * LANES:(g + 1) * LANES]


def _in_proj(x2d, gain, w_bf16, col_scale):
    n_tok = x2d.shape[0]
    n_out = w_bf16.shape[1]
    n_groups = n_out // LANES
    return pl.pallas_call(
        _in_proj_kernel,
        grid=(n_tok // TOKEN_TILE,),
        in_specs=[
            pl.BlockSpec((TOKEN_TILE, D_MODEL), lambda i: (i, 0)),
            pl.BlockSpec((1, D_MODEL), lambda i: (0, 0)),
            pl.BlockSpec((D_MODEL, n_out), lambda i: (0, 0)),
            pl.BlockSpec((1, n_out), lambda i: (0, 0)),
        ],
        out_specs=pl.BlockSpec((n_groups, TOKEN_TILE, LANES), lambda i: (0, i, 0)),
        out_shape=jax.ShapeDtypeStruct((n_groups, n_tok, LANES), _BF16),
        compiler_params=pltpu.CompilerParams(vmem_limit_bytes=VMEM_LIMIT_BYTES),
        name="in_proj",
    )(x2d, gain, w_bf16, col_scale)


def _split_halves(x_bf16):
    lane = lax.broadcasted_iota(jnp.int32, (1, LANES), 1)
    zero = jnp.zeros_like(x_bf16)
    return (jnp.where(lane < HEAD_DIM, x_bf16, zero), jnp.where(lane >= HEAD_DIM, x_bf16, zero))


def _rows(blk):
    start = blk * ATTN_TILE
    if not isinstance(start, int):
        start = pl.multiple_of(start, ATTN_TILE)
    return pl.ds(start, ATTN_TILE)


def _as_f32(i):
    return jnp.asarray(i, jnp.int32).astype(_F32)


def _group_spec(seq, section):
    return pl.BlockSpec((GROUPS, seq, LANES), lambda bi: (section, bi, 0))


def _offdiag_steps(n_blk):
    steps = [(qi, kj) for qi in range(1, n_blk) for kj in range(qi - 1, -1, -1)]
    return (jnp.asarray(np.array([qk[0] for qk in steps], np.int32)),
            jnp.asarray(np.array([qk[1] for qk in steps], np.int32)))


def _skewed_pipeline(n_units, stages):
    depth = len(stages)
    assert n_units >= depth

    def iteration(j, parity, lo, hi):
        for k in range(depth - 1, -1, -1):
            if lo <= k < hi:
                stages[k](j - k, (parity - k) % 2)

    for j in range(depth - 1):
        iteration(j, j % 2, 0, j + 1)

    start = depth - 1
    n_pairs, odd = divmod(n_units - start, 2)

    def body(i, c):
        j = start + 2 * i
        iteration(j, start % 2, 0, depth)
        iteration(j + 1, (start + 1) % 2, 0, depth)
        return c

    lax.fori_loop(0, n_pairs, body, 0)
    if odd:
        iteration(n_units - 1, (n_units - 1) % 2, 0, depth)
    for j in range(n_units, n_units + depth - 1):
        iteration(j, j % 2, j - n_units + 1, depth)


def _diff_attn_kernel(steps_q_ref, steps_k_ref, slopes_ref, lq1_ref, lk1_ref, lq2_ref, lk2_ref,
                      g_ref, q_ref, k_ref, v_ref, o_ref,
                      m_ref, a_ref, s_ref, p_ref, al_ref, dbias_ref):
    t = ATTN_TILE
    reps = t // LANES
    n_blk = q_ref.shape[1] // t
    n_h = q_ref.shape[0]
    h_shift = n_h.bit_length() - 1
    assert n_h == 1 << h_shift

    lam = (jnp.exp(jnp.sum(lq1_ref[...] * lk1_ref[...], axis=-1, keepdims=True))
           - jnp.exp(jnp.sum(lq2_ref[...] * lk2_ref[...], axis=-1, keepdims=True))
           + LAM_INIT)

    assert n_blk % 2 == 0
    wide = 2 * t
    row = lax.broadcasted_iota(jnp.int32, (t, t), 0)
    col = lax.broadcasted_iota(jnp.int32, (t, t), 1)
    colf = lax.broadcasted_iota(jnp.int32, (1, wide), 1).astype(_F32)
    for h in range(n_h):
        diag = jnp.where(row >= col, slopes_ref[h] * col.astype(_F32), MASKED)
        dbias_ref[h, :, t:] = diag
        dbias_ref[h, :, :t] = jnp.broadcast_to(slopes_ref[h] * (colf[:, :t] - t), (t, t))

    def make_stages(kind):
        first = kind != "full"
        width = t if kind == "diag" else wide
        ones = jnp.ones((width, LANES), _BF16)

        def decode(u):
            step = u >> h_shift
            head = u & (n_h - 1)
            if kind == "diag":
                return 2 * step, 2 * step * t, head
            if kind == "pair_diag":
                return 2 * step + 1, 2 * step * t, head
            return steps_q_ref[step], steps_k_ref[step] * wide, head

        def key_rows(start):
            if not isinstance(start, int):
                start = pl.multiple_of(start, t)
            return pl.ds(start, width)

        def scores(u, slot):
            qi, k0, head = decode(u)
            q_halves = _split_halves(q_ref[head, _rows(qi), :])
            k = k_ref[head, key_rows(k0), :]
            if kind == "diag":
                bias = dbias_ref[head, :, t:]
            elif kind == "pair_diag":
                bias = dbias_ref[head]
            else:
                bias = slopes_ref[head] * (colf + _as_f32(k0 - qi * t))
            for half in range(2):
                s_ref[slot, half, :, :width] = _dot_nt(q_halves[half], k) + bias

        def probabilities(u, slot):
            qi, k0, head = decode(u)
            for half in range(2):
                idx = 2 * head + half
                s = s_ref[slot, half, :, :width]
                m_curr = jnp.max(s, axis=1, keepdims=True)
                if first:
                    m_new = jnp.broadcast_to(m_curr, (t, LANES))
                else:
                    m_prev = m_ref[qi, idx]
                    m_new = jnp.maximum(m_prev, m_curr)
                    al_ref[slot, half] = jnp.exp2(m_prev - m_new)
                p = jnp.exp2(s - _lane_tile(m_new, width // LANES))
                p_ref[slot, half, :, :width] = p.astype(_BF16)
                m_ref[qi, idx] = m_new

        def values(u, slot):
            qi, k0, head = decode(u)
            v_aug = jnp.concatenate([v_ref[head, key_rows(k0), :], ones], axis=1)
            for half in range(2):
                idx = 2 * head + half
                pv = _dot(p_ref[slot, half, :, :width], v_aug)
                if first:
                    a_ref[qi, idx] = pv
                else:
                    a_ref[qi, idx] = _lane_tile(al_ref[slot, half], 2) * a_ref[qi, idx] + pv

        return scores, probabilities, values

    _skewed_pipeline(n_blk // 2 * n_h, make_stages("diag"))
    _skewed_pipeline(n_blk // 2 * n_h, make_stages("pair_diag"))
    _skewed_pipeline(steps_q_ref.shape[0] * n_h, make_stages("full"))

    def finish(qi, carry):
        for h in range(n_h):
            a1 = a_ref[qi, 2 * h]
            a2 = a_ref[qi, 2 * h + 1]
            o = a1[:, :LANES] / a1[:, LANES:] - lam * (a2[:, :LANES] / a2[:, LANES:])
            o = _rmsnorm(o, g_ref[...]) * (1.0 - LAM_INIT)
            o_ref[h, _rows(qi), :] = o.astype(o_ref.dtype)
        return carry

    lax.fori_loop(0, n_blk, finish, 0)


def _diff_attn(qkv_g, batch, slopes2, lq1, lk1, lq2, lk2, gain):
    n_tok = qkv_g.shape[1]
    s = n_tok // batch
    n_h = N_DIFF_HEADS
    n_blk = s // ATTN_TILE
    steps = [(qi, c) for qi in range(2, n_blk) for c in range(qi // 2)]
    steps_q = jnp.asarray(np.array([qc[0] for qc in steps], np.int32))
    steps_k = jnp.asarray(np.array([qc[1] for qc in steps], np.int32))
    smem = pl.BlockSpec(memory_space=pltpu.SMEM)
    vec = lambda n: pl.BlockSpec((1, n), lambda bi: (0, 0))
    handoff = lambda width, dtype: pltpu.VMEM((2, 2, ATTN_TILE, width), dtype)
    return pl.pallas_call(
        _diff_attn_kernel,
        grid=(batch,),
        in_specs=[
            smem, smem, smem,
            vec(HEAD_DIM), vec(HEAD_DIM), vec(HEAD_DIM), vec(HEAD_DIM), vec(LANES),
            _group_spec(s, 0), _group_spec(s, 1), _group_spec(s, 2),
        ],
        out_specs=pl.BlockSpec((GROUPS, s, LANES), lambda bi: (0, bi, 0)),
        out_shape=jax.ShapeDtypeStruct((GROUPS, n_tok, LANES), _BF16),
        scratch_shapes=[pltpu.VMEM((n_blk, 2 * n_h, ATTN_TILE, LANES), _F32),
                        pltpu.VMEM((n_blk, 2 * n_h, ATTN_TILE, 2 * LANES), _F32),
                        handoff(2 * ATTN_TILE, _F32), handoff(2 * ATTN_TILE, _BF16),
                        handoff(LANES, _F32),
                        pltpu.VMEM((n_h, ATTN_TILE, 2 * ATTN_TILE), _F32)],
        compiler_params=pltpu.CompilerParams(vmem_limit_bytes=VMEM_LIMIT_BYTES),
        name="diff_attn",
    )(steps_q, steps_k, slopes2, lq1, lk1, lq2, lk2, gain, qkv_g, qkv_g, qkv_g)


def _sb_attn_kernel(steps_q_ref, steps_k_ref, g_ref, q_ref, k_ref, v_ref, o_ref,
                    c_ref, acc_ref, z_ref, x_ref, lb_ref, w_ref, tri_ref, dmask_ref,
                    qm_ref, vm_ref):
    t = ATTN_TILE
    reps = t // LANES
    n_blk = q_ref.shape[1] // t
    n_g = q_ref.shape[0]
    g_shift = n_g.bit_length() - 1
    assert n_g == 1 << g_shift

    row = lax.broadcasted_iota(jnp.int32, (t, t), 0)
    col = lax.broadcasted_iota(jnp.int32, (t, t), 1)
    tri_ref[...] = jnp.where(row > col, -1.0, 0.0).astype(_BF16)
    dmask_ref[...] = jnp.where(col < row, 0.0, MASKED)

    def split_block(j, c):
        for grp in range(n_g):
            for src_ref, dst_ref in ((q_ref, qm_ref), (v_ref, vm_ref)):
                halves = _split_halves(src_ref[grp, _rows(j), :])
                dst_ref[grp, 0, _rows(j), :] = halves[0]
                dst_ref[grp, 1, _rows(j), :] = halves[1]
        return c

    lax.fori_loop(0, n_blk, split_block, 0)

    def make_stages(diagonal):
        def decode(u):
            step = u >> g_shift
            grp = u & (n_g - 1)
            if diagonal:
                return step, step, grp
            return steps_q_ref[step], steps_k_ref[step], grp

        def logits(u, slot):
            qi, kj, grp = decode(u)
            k = k_ref[grp, _rows(kj), :]
            for half in range(2):
                z = _dot_nt(qm_ref[grp, half, _rows(qi), :], k)
                if diagonal:
                    z = z + dmask_ref[...]
                z_ref[slot, half] = z

        def log_terms(u, slot):
            qi, kj, grp = decode(u)
            for half in range(2):
                head = 2 * grp + half
                z = z_ref[slot, half]
                sp = jnp.log(1.0 + jnp.exp2(-jnp.abs(z))) * LOG2E
                neg_log_om = jnp.maximum(z, 0.0) + sp
                log_beta = z - neg_log_om
                om_sum = jnp.sum(neg_log_om, axis=1, keepdims=True)
                x_ref[slot, half] = neg_log_om.astype(_BF16)
                if diagonal:
                    lb_ref[slot, half] = log_beta
                    c_ref[qi, head] = jnp.broadcast_to(om_sum, (t, LANES))
                else:
                    carry = c_ref[qi, head]
                    lb_ref[slot, half] = log_beta - _lane_tile(carry, reps)
                    c_ref[qi, head] = carry + om_sum

        def weights(u, slot):
            for half in range(2):
                later = _dot(x_ref[slot, half], tri_ref[...])
                w_ref[slot, half] = jnp.exp2(later + lb_ref[slot, half]).astype(_BF16)

        def values(u, slot):
            qi, kj, grp = decode(u)
            pv = (_dot(w_ref[slot, 0], vm_ref[grp, 0, _rows(kj), :])
                  + _dot(w_ref[slot, 1], vm_ref[grp, 1, _rows(kj), :]))
            if diagonal:
                acc_ref[qi, grp] = pv
            else:
                acc_ref[qi, grp] += pv

        return logits, log_terms, weights, values

    _skewed_pipeline(n_blk * n_g, make_stages(True))
    _skewed_pipeline(steps_q_ref.shape[0] * n_g, make_stages(False))

    li = lax.broadcasted_iota(jnp.int32, (LANES, LANES), 0)
    lj = lax.broadcasted_iota(jnp.int32, (LANES, LANES), 1)
    head_mean = jnp.where((li < HEAD_DIM) == (lj < HEAD_DIM), 1.0 / HEAD_DIM, 0.0).astype(_BF16)

    def finish(qi, carry):
        for grp in range(n_g):
            o = acc_ref[qi, grp]
            ms = _dot((o * o).astype(_BF16), head_mean)
            o_ref[grp, _rows(qi), :] = (o * lax.rsqrt(ms + EPS) * g_ref[...]).astype(o_ref.dtype)
        return carry

    lax.fori_loop(0, n_blk, finish, 0)


def _sb_attn(qkv_g, batch, gain2):
    n_tok = qkv_g.shape[1]
    s = n_tok // batch
    n_blk = s // ATTN_TILE
    steps_q, steps_k = _offdiag_steps(n_blk)
    smem = pl.BlockSpec(memory_space=pltpu.SMEM)
    return pl.pallas_call(
        _sb_attn_kernel,
        grid=(batch,),
        in_specs=[
            smem, smem,
            pl.BlockSpec((1, LANES), lambda bi: (0, 0)),
            _group_spec(s, 3), _group_spec(s, 4), _group_spec(s, 5),
        ],
        out_specs=pl.BlockSpec((GROUPS, s, LANES), lambda bi: (0, bi, 0)),
        out_shape=jax.ShapeDtypeStruct((GROUPS, n_tok, LANES), _BF16),
        scratch_shapes=[pltpu.VMEM((n_blk, N_SB_HEADS, ATTN_TILE, LANES), _F32),
                        pltpu.VMEM((n_blk, GROUPS, ATTN_TILE, LANES), _F32),
                        pltpu.VMEM((2, 2, ATTN_TILE, ATTN_TILE), _F32),
                        pltpu.VMEM((2, 2, ATTN_TILE, ATTN_TILE), _BF16),
                        pltpu.VMEM((2, 2, ATTN_TILE, ATTN_TILE), _F32),
                        pltpu.VMEM((2, 2, ATTN_TILE, ATTN_TILE), _BF16),
                        pltpu.VMEM((ATTN_TILE, ATTN_TILE), _BF16),
                        pltpu.VMEM((ATTN_TILE, ATTN_TILE), _F32),
                        pltpu.VMEM((GROUPS, 2, s, LANES), _BF16),
                        pltpu.VMEM((GROUPS, 2, s, LANES), _BF16)],
        compiler_params=pltpu.CompilerParams(vmem_limit_bytes=VMEM_LIMIT_BYTES),
        name="sb_attn",
    )(steps_q, steps_k, gain2, qkv_g, qkv_g, qkv_g)


def _out_mlp_kernel(x_ref, md_ref, ms_ref, wo_ref, g2_ref, wup_ref, wdn_ref,
                    g3_ref, o_ref, m_ref, acc_ref):
    mixed = jnp.concatenate([md_ref[g] for g in range(md_ref.shape[0])]
                            + [ms_ref[g] for g in range(ms_ref.shape[0])], axis=1)
    h = x_ref[...] + _dot(mixed, wo_ref[...])
    m_ref[...] = _rmsnorm(h, g2_ref[...]).astype(_BF16)
    acc_ref[...] = h

    def ff_chunk(c, carry):
        cols = pl.ds(pl.multiple_of(c * FF_CHUNK, FF_CHUNK), FF_CHUNK)
        u = jnp.square(jnp.maximum(_dot(m_ref[...], wup_ref[:, cols]), 0.0))
        acc_ref[...] += _dot(u.astype(_BF16), wdn_ref[cols, :])
        return carry

    lax.fori_loop(0, D_FF // FF_CHUNK, ff_chunk, 0)
    o_ref[...] = _rmsnorm(acc_ref[...], g3_ref[...])


def _out_mlp(x2d, mix_d, mix_s, w_out, g2, w_up, w_dn, g3):
    n_tok = x2d.shape[0]
    const = lambda shape: pl.BlockSpec(shape, lambda i: (0, 0), pipeline_mode=pl.Buffered(1))
    tile = MLP_TOKEN_TILE
    mix = pl.BlockSpec((GROUPS, tile, LANES), lambda i: (0, i, 0))
    return pl.pallas_call(
        _out_mlp_kernel,
        grid=(n_tok // tile,),
        in_specs=[
            pl.BlockSpec((tile, D_MODEL), lambda i: (i, 0)),
            mix, mix,
            const((MIX_WIDTH, D_MODEL)),
            const((1, D_MODEL)),
            const((D_MODEL, D_FF)),
            const((D_FF, D_MODEL)),
            const((1, D_MODEL)),
        ],
        out_specs=pl.BlockSpec((tile, D_MODEL), lambda i: (i, 0)),
        out_shape=jax.ShapeDtypeStruct((n_tok, D_MODEL), _F32),
        scratch_shapes=[pltpu.VMEM((tile, D_MODEL), _BF16),
                        pltpu.VMEM((tile, D_MODEL), _F32)],
        compiler_params=pltpu.CompilerParams(vmem_limit_bytes=VMEM_LIMIT_BYTES),
        name="out_mlp",
    )(x2d, mix_d, mix_s, w_out, g2, w_up, w_dn, g3)


def _query_col_scale():
    qs = ATTN_SCALE * LOG2E
    one = jnp.ones((DIFF_WIDTH,), _F32)
    return jnp.concatenate([qs * one, one, one, qs * one, one, one]).reshape(1, 3 * MIX_WIDTH)


def kernel(x, attn_norm, w_in, lambda_q1, lambda_k1, lambda_q2, lambda_k2, diff_subln, sb_subln,
           w_out, mlp_norm, w_up, w_down, final_norm):
    assert attn_norm.shape[0] == 1, "single-layer block"
    b, s, d = x.shape
    x2d = x.reshape(b * s, d)

    qkv_g = _in_proj(x2d, attn_norm, w_in[0].astype(_BF16), _query_col_scale())

    slopes2 = jnp.asarray([LOG2E * 2.0 ** (-8.0 * (h + 1) / N_DIFF_HEADS)
                           for h in range(N_DIFF_HEADS)], dtype=_F32)
    mix_d = _diff_attn(qkv_g, b, slopes2, lambda_q1, lambda_k1, lambda_q2, lambda_k2, diff_subln)
    mix_s = _sb_attn(qkv_g, b, jnp.tile(sb_subln, (1, 2)))

    out = _out_mlp(x2d, mix_d, mix_s, w_out[0].astype(_BF16), mlp_norm,
                   w_up[0].astype(_BF16), w_down[0].astype(_BF16), final_norm.reshape(1, d))
    return out.reshape(b, s, d)
```

```python
import math

import jax
import jax.numpy as jnp
import numpy as np
from jax import lax
from jax.experimental import pallas as pl
from jax.experimental.pallas import tpu as pltpu

D_MODEL = 1024
HEAD_DIM = 64
DIFF_WIDTH = D_MODEL // 2
N_DIFF_HEADS = DIFF_WIDTH // (2 * HEAD_DIM)
SB_WIDTH = D_MODEL - DIFF_WIDTH
N_SB_HEADS = SB_WIDTH // HEAD_DIM
N_SB_PAIRS = N_SB_HEADS // 2
MIX_WIDTH = DIFF_WIDTH + SB_WIDTH
D_FF = 4 * D_MODEL
EPS = 1e-6
LAYER_IDX = 0
LAM_INIT = 0.8 - 0.6 * math.exp(-0.3 * LAYER_IDX)
ATTN_SCALE = 1.0 / math.sqrt(HEAD_DIM)
LOG2E = math.log2(math.e)

LANES = 128
GROUPS = 4
ATTN_TILE = 256
TOKEN_TILE = 1024
MLP_TOKEN_TILE = 1024
FF_CHUNK = 1024
MASKED = -1e30
VMEM_LIMIT_BYTES = 56 * 1024 * 1024

_F32 = jnp.float32
_BF16 = jnp.bfloat16


def _rmsnorm(x, gain):
    inv = lax.rsqrt(jnp.mean(x * x, axis=-1, keepdims=True) + EPS)
    return x * inv * gain


def _dot(a, b):
    return jnp.dot(a, b, preferred_element_type=_F32)


def _dot_nt(a, b):
    return lax.dot_general(a, b, (((1,), (1,)), ((), ())), preferred_element_type=_F32)


def _lane_tile(x, reps):
    return jnp.concatenate([x] * reps, axis=1)


def _in_proj_kernel(x_ref, g_ref, w_ref, cs_ref, o_ref):
    a = _rmsnorm(x_ref[...], g_ref[...])
    res = (_dot(a.astype(_BF16), w_ref[...]) * cs_ref[...]).astype(_BF16)
    for g in range(o_ref.shape[0]):
        o_ref[g] = res[:, g * LANES:(g + 1) * LANES]


def _in_proj(x2d, gain, w_bf16, col_scale):
    n_tok = x2d.shape[0]
    n_out = w_bf16.shape[1]
    n_groups = n_out // LANES
    return pl.pallas_call(
        _in_proj_kernel,
        grid=(n_tok // TOKEN_TILE,),
        in_specs=[
            pl.BlockSpec((TOKEN_TILE, D_MODEL), lambda i: (i, 0)),
            pl.BlockSpec((1, D_MODEL), lambda i: (0, 0)),
            pl.BlockSpec((D_MODEL, n_out), lambda i: (0, 0)),
            pl.BlockSpec((1, n_out), lambda i: (0, 0)),
        ],
        out_specs=pl.BlockSpec((n_groups, TOKEN_TILE, LANES), lambda i: (0, i, 0)),
        out_shape=jax.ShapeDtypeStruct((n_groups, n_tok, LANES), _BF16),
        compiler_params=pltpu.CompilerParams(vmem_limit_bytes=VMEM_LIMIT_BYTES),
        name="in_proj",
    )(x2d, gain, w_bf16, col_scale)


def _split_halves(x_bf16):
    lane = lax.broadcasted_iota(jnp.int32, (1, LANES), 1)
    zero = jnp.zeros_like(x_bf16)
    return (jnp.where(lane < HEAD_DIM, x_bf16, zero), jnp.where(lane >= HEAD_DIM, x_bf16, zero))


def _rows(blk):
    start = blk * ATTN_TILE
    if not isinstance(start, int):
        start = pl.multiple_of(start, ATTN_TILE)
    return pl.ds(start, ATTN_TILE)


def _as_f32(i):
    return jnp.asarray(i, jnp.int32).astype(_F32)


def _group_spec(seq, section):
    return pl.BlockSpec((GROUPS, seq, LANES), lambda bi: (section, bi, 0))


def _offdiag_steps(n_blk):
    steps = [(qi, kj) for qi in range(1, n_blk) for kj in range(qi - 1, -1, -1)]
    return (jnp.asarray(np.array([qk[0] for qk in steps], np.int32)),
            jnp.asarray(np.array([qk[1] for qk in steps], np.int32)))


def _skewed_pipeline(n_units, stages):
    depth = len(stages)
    assert n_units >= depth

    def iteration(j, parity, lo, hi):
        for k in range(depth - 1, -1, -1):
            if lo <= k < hi:
                stages[k](j - k, (parity - k) % 2)

    for j in range(depth - 1):
        iteration(j, j % 2, 0, j + 1)

    start = depth - 1
    n_pairs, odd = divmod(n_units - start, 2)

    def body(i, c):
        j = start + 2 * i
        iteration(j, start % 2, 0, depth)
        iteration(j + 1, (start + 1) % 2, 0, depth)
        return c

    lax.fori_loop(0, n_pairs, body, 0)
    if odd:
        iteration(n_units - 1, (n_units - 1) % 2, 0, depth)
    for j in range(n_units, n_units + depth - 1):
        iteration(j, j % 2, j - n_units + 1, depth)


def _diff_attn_kernel(slopes_ref, lq1_ref, lk1_ref, lq2_ref, lk2_ref,
                      g_ref, q_ref, k_ref, v_ref, o_ref, m_ref, a_ref, dbias_ref):
    t = ATTN_TILE
    n_blk = q_ref.shape[1] // t
    n_h = q_ref.shape[0]

    lam = (jnp.exp(jnp.sum(lq1_ref[...] * lk1_ref[...], axis=-1, keepdims=True))
           - jnp.exp(jnp.sum(lq2_ref[...] * lk2_ref[...], axis=-1, keepdims=True))
           + LAM_INIT)

    assert n_blk % 2 == 0
    wide = 2 * t
    row = lax.broadcasted_iota(jnp.int32, (t, t), 0)
    col = lax.broadcasted_iota(jnp.int32, (t, t), 1)
    colf = lax.broadcasted_iota(jnp.int32, (1, wide), 1).astype(_F32)
    for h in range(n_h):
        diag = jnp.where(row >= col, slopes_ref[h] * col.astype(_F32), MASKED)
        dbias_ref[h, :, t:] = diag
        dbias_ref[h, :, :t] = jnp.broadcast_to(slopes_ref[h] * (colf[:, :t] - t), (t, t))

    def key_step(kind, qi, k0):
        first = kind != "full"
        width = t if kind == "diag" else wide
        ones = jnp.ones((width, LANES), _BF16)
        if not isinstance(k0, int):
            k0 = pl.multiple_of(k0, t)
        for head in range(n_h):
            q_halves = _split_halves(q_ref[head, _rows(qi), :])
            k = k_ref[head, pl.ds(k0, width), :]
            v_aug = jnp.concatenate([v_ref[head, pl.ds(k0, width), :], ones], axis=1)
            if kind == "diag":
                bias = dbias_ref[head, :, t:]
            elif kind == "pair_diag":
                bias = dbias_ref[head]
            else:
                bias = slopes_ref[head] * (colf + _as_f32(k0 - qi * t))
            for half in range(2):
                idx = 2 * head + half
                s = _dot_nt(q_halves[half], k) + bias
                m_curr = jnp.max(s, axis=1, keepdims=True)
                if first:
                    m_new = jnp.broadcast_to(m_curr, (t, LANES))
                else:
                    m_prev = m_ref[idx]
                    m_new = jnp.maximum(m_prev, m_curr)
                p = jnp.exp2(s - _lane_tile(m_new, width // LANES))
                pv = _dot(p.astype(_BF16), v_aug)
                if first:
                    a_ref[idx] = pv
                else:
                    a_ref[idx] = _lane_tile(jnp.exp2(m_prev - m_new), 2) * a_ref[idx] + pv
                m_ref[idx] = m_new

    def finish(qi):
        for h in range(n_h):
            a1 = a_ref[2 * h]
            a2 = a_ref[2 * h + 1]
            o = a1[:, :LANES] / a1[:, LANES:] - lam * (a2[:, :LANES] / a2[:, LANES:])
            o = _rmsnorm(o, g_ref[...]) * (1.0 - LAM_INIT)
            o_ref[h, _rows(qi), :] = o.astype(o_ref.dtype)

    def query_block_pair(r, carry):
        for parity, first_kind in ((0, "diag"), (1, "pair_diag")):
            qi = 2 * r + parity
            key_step(first_kind, qi, 2 * r * t)

            def full_chunk(c, cc, qi=qi):
                key_step("full", qi, c * wide)
                return cc

            lax.fori_loop(0, r, full_chunk, 0)
            finish(qi)
        return carry

    lax.fori_loop(0, n_blk // 2, query_block_pair, 0)


def _diff_attn(qkv_g, batch, slopes2, lq1, lk1, lq2, lk2, gain):
    n_tok = qkv_g.shape[1]
    s = n_tok // batch
    n_h = N_DIFF_HEADS
    vec = lambda n: pl.BlockSpec((1, n), lambda bi: (0, 0))
    return pl.pallas_call(
        _diff_attn_kernel,
        grid=(batch,),
        in_specs=[
            pl.BlockSpec(memory_space=pltpu.SMEM),
            vec(HEAD_DIM), vec(HEAD_DIM), vec(HEAD_DIM), vec(HEAD_DIM), vec(LANES),
            _group_spec(s, 0), _group_spec(s, 1), _group_spec(s, 2),
        ],
        out_specs=pl.BlockSpec((GROUPS, s, LANES), lambda bi: (0, bi, 0)),
        out_shape=jax.ShapeDtypeStruct((GROUPS, n_tok, LANES), _BF16),
        scratch_shapes=[pltpu.VMEM((2 * n_h, ATTN_TILE, LANES), _F32),
                        pltpu.VMEM((2 * n_h, ATTN_TILE, 2 * LANES), _F32),
                        pltpu.VMEM((n_h, ATTN_TILE, 2 * ATTN_TILE), _F32)],
        compiler_params=pltpu.CompilerParams(vmem_limit_bytes=VMEM_LIMIT_BYTES),
        name="diff_attn",
    )(slopes2, lq1, lk1, lq2, lk2, gain, qkv_g, qkv_g, qkv_g)


def _sb_attn_kernel(steps_q_ref, steps_k_ref, g_ref, q_ref, k_ref, v_ref, o_ref,
                    c_ref, acc_ref, z_ref, x_ref, lb_ref, w_ref, tri_ref, dmask_ref,
                    qm_ref, vm_ref):
    t = ATTN_TILE
    reps = t // LANES
    n_blk = q_ref.shape[1] // t
    n_g = q_ref.shape[0]
    g_shift = n_g.bit_length() - 1
    assert n_g == 1 << g_shift

    row = lax.broadcasted_iota(jnp.int32, (t, t), 0)
    col = lax.broadcasted_iota(jnp.int32, (t, t), 1)
    tri_ref[...] = jnp.where(row > col, -1.0, 0.0).astype(_BF16)
    dmask_ref[...] = jnp.where(col < row, 0.0, MASKED)

    def split_block(j, c):
        for grp in range(n_g):
            for src_ref, dst_ref in ((q_ref, qm_ref), (v_ref, vm_ref)):
                halves = _split_halves(src_ref[grp, _rows(j), :])
                dst_ref[grp, 0, _rows(j), :] = halves[0]
                dst_ref[grp, 1, _rows(j), :] = halves[1]
        return c

    lax.fori_loop(0, n_blk, split_block, 0)

    def make_stages(diagonal):
        def decode(u):
            step = u >> g_shift
            grp = u & (n_g - 1)
            if diagonal:
                return step, step, grp
            return steps_q_ref[step], steps_k_ref[step], grp

        def logits(u, slot):
            qi, kj, grp = decode(u)
            k = k_ref[grp, _rows(kj), :]
            for half in range(2):
                z = _dot_nt(qm_ref[grp, half, _rows(qi), :], k)
                if diagonal:
                    z = z + dmask_ref[...]
                z_ref[slot, half] = z

        def log_terms(u, slot):
            qi, kj, grp = decode(u)
            for half in range(2):
                head = 2 * grp + half
                z = z_ref[slot, half]
                sp = jnp.log(1.0 + jnp.exp2(-jnp.abs(z))) * LOG2E
                neg_log_om = jnp.maximum(z, 0.0) + sp
                log_beta = z - neg_log_om
                om_sum = jnp.sum(neg_log_om, axis=1, keepdims=True)
                x_ref[slot, half] = neg_log_om.astype(_BF16)
                if diagonal:
                    lb_ref[slot, half] = log_beta
                    c_ref[qi, head] = jnp.broadcast_to(om_sum, (t, LANES))
                else:
                    carry = c_ref[qi, head]
                    lb_ref[slot, half] = log_beta - _lane_tile(carry, reps)
                    c_ref[qi, head] = carry + om_sum

        def weights(u, slot):
            for half in range(2):
                later = _dot(x_ref[slot, half], tri_ref[...])
                w_ref[slot, half] = jnp.exp2(later + lb_ref[slot, half]).astype(_BF16)

        def values(u, slot):
            qi, kj, grp = decode(u)
            pv = (_dot(w_ref[slot, 0], vm_ref[grp, 0, _rows(kj), :])
                  + _dot(w_ref[slot, 1], vm_ref[grp, 1, _rows(kj), :]))
            if diagonal:
                acc_ref[qi, grp] = pv
            else:
                acc_ref[qi, grp] += pv

        return logits, log_terms, weights, values

    _skewed_pipeline(n_blk * n_g, make_stages(True))
    _skewed_pipeline(steps_q_ref.shape[0] * n_g, make_stages(False))

    li = lax.broadcasted_iota(jnp.int32, (LANES, LANES), 0)
    lj = lax.broadcasted_iota(jnp.int32, (LANES, LANES), 1)
    head_mean = jnp.where((li < HEAD_DIM) == (lj < HEAD_DIM), 1.0 / HEAD_DIM, 0.0).astype(_BF16)

    def finish(qi, carry):
        for grp in range(n_g):
            o = acc_ref[qi, grp]
            ms = _dot((o * o).astype(_BF16), head_mean)
            o_ref[grp, _rows(qi), :] = (o * lax.rsqrt(ms + EPS) * g_ref[...]).astype(o_ref.dtype)
        return carry

    lax.fori_loop(0, n_blk, finish, 0)


def _sb_attn(qkv_g, batch, gain2):
    n_tok = qkv_g.shape[1]
    s = n_tok // batch
    n_blk = s // ATTN_TILE
    steps_q, steps_k = _offdiag_steps(n_blk)
    smem = pl.BlockSpec(memory_space=pltpu.SMEM)
    return pl.pallas_call(
        _sb_attn_kernel,
        grid=(batch,),
        in_specs=[
            smem, smem,
            pl.BlockSpec((1, LANES), lambda bi: (0, 0)),
            _group_spec(s, 3), _group_spec(s, 4), _group_spec(s, 5),
        ],
        out_specs=pl.BlockSpec((GROUPS, s, LANES), lambda bi: (0, bi, 0)),
        out_shape=jax.ShapeDtypeStruct((GROUPS, n_tok, LANES), _BF16),
        scratch_shapes=[pltpu.VMEM((n_blk, N_SB_HEADS, ATTN_TILE, LANES), _F32),
                        pltpu.VMEM((n_blk, GROUPS, ATTN_TILE, LANES), _F32),
                        pltpu.VMEM((2, 2, ATTN_TILE, ATTN_TILE), _F32),
                        pltpu.VMEM((2, 2, ATTN_TILE, ATTN_TILE), _BF16),
                        pltpu.VMEM((2, 2, ATTN_TILE, ATTN_TILE), _F32),
                        pltpu.VMEM((2, 2, ATTN_TILE, ATTN_TILE), _BF16),
                        pltpu.VMEM((ATTN_TILE, ATTN_TILE), _BF16),
                        pltpu.VMEM((ATTN_TILE, ATTN_TILE), _F32),
                        pltpu.VMEM((GROUPS, 2, s, LANES), _BF16),
                        pltpu.VMEM((GROUPS, 2, s, LANES), _BF16)],
        compiler_params=pltpu.CompilerParams(vmem_limit_bytes=VMEM_LIMIT_BYTES),
        name="sb_attn",
    )(steps_q, steps_k, gain2, qkv_g, qkv_g, qkv_g)


def _out_mlp_kernel(x_ref, md_ref, ms_ref, wo_ref, g2_ref, wup_ref, wdn_ref,
                    g3_ref, o_ref, m_ref, acc_ref):
    mixed = jnp.concatenate([md_ref[g] for g in range(md_ref.shape[0])]
                            + [ms_ref[g] for g in range(ms_ref.shape[0])], axis=1)
    h = x_ref[...] + _dot(mixed, wo_ref[...])
    m_ref[...] = _rmsnorm(h, g2_ref[...]).astype(_BF16)
    acc_ref[...] = h

    def ff_chunk(c, carry):
        cols = pl.ds(pl.multiple_of(c * FF_CHUNK, FF_CHUNK), FF_CHUNK)
        u = jnp.square(jnp.maximum(_dot(m_ref[...], wup_ref[:, cols]), 0.0))
        acc_ref[...] += _dot(u.astype(_BF16), wdn_ref[cols, :])
        return carry

    lax.fori_loop(0, D_FF // FF_CHUNK, ff_chunk, 0)
    o_ref[...] = _rmsnorm(acc_ref[...], g3_ref[...])


def _out_mlp(x2d, mix_d, mix_s, w_out, g2, w_up, w_dn, g3):
    n_tok = x2d.shape[0]
    const = lambda shape: pl.BlockSpec(shape, lambda i: (0, 0), pipeline_mode=pl.Buffered(1))
    tile = MLP_TOKEN_TILE
    mix = pl.BlockSpec((GROUPS, tile, LANES), lambda i: (0, i, 0))
    return pl.pallas_call(
        _out_mlp_kernel,
        grid=(n_tok // tile,),
        in_specs=[
            pl.BlockSpec((tile, D_MODEL), lambda i: (i, 0)),
            mix, mix,
            const((MIX_WIDTH, D_MODEL)),
            const((1, D_MODEL)),
            const((D_MODEL, D_FF)),
            const((D_FF, D_MODEL)),
            const((1, D_MODEL)),
        ],
        out_specs=pl.BlockSpec((tile, D_MODEL), lambda i: (i, 0)),
        out_shape=jax.ShapeDtypeStruct((n_tok, D_MODEL), _F32),
        scratch_shapes=[pltpu.VMEM((tile, D_MODEL), _BF16),
                        pltpu.VMEM((tile, D_MODEL), _F32)],
        compiler_params=pltpu.CompilerParams(vmem_limit_bytes=VMEM_LIMIT_BYTES),
        name="out_mlp",
    )(x2d, mix_d, mix_s, w_out, g2, w_up, w_dn, g3)


def _query_col_scale():
    qs = ATTN_SCALE * LOG2E
    one = jnp.ones((DIFF_WIDTH,), _F32)
    return jnp.concatenate([qs * one, one, one, qs * one, one, one]).reshape(1, 3 * MIX_WIDTH)


def kernel(x, attn_norm, w_in, lambda_q1, lambda_k1, lambda_q2, lambda_k2, diff_subln, sb_subln,
           w_out, mlp_norm, w_up, w_down, final_norm):
    assert attn_norm.shape[0] == 1, "single-layer block"
    b, s, d = x.shape
    x2d = x.reshape(b * s, d)

    qkv_g = _in_proj(x2d, attn_norm, w_in[0].astype(_BF16), _query_col_scale())

    slopes2 = jnp.asarray([LOG2E * 2.0 ** (-8.0 * (h + 1) / N_DIFF_HEADS)
                           for h in range(N_DIFF_HEADS)], dtype=_F32)
    mix_d = _diff_attn(qkv_g, b, slopes2, lambda_q1, lambda_k1, lambda_q2, lambda_k2, diff_subln)
    mix_s = _sb_attn(qkv_g, b, jnp.tile(sb_subln, (1, 2)))

    out = _out_mlp(x2d, mix_d, mix_s, w_out[0].astype(_BF16), mlp_norm,
                   w_up[0].astype(_BF16), w_down[0].astype(_BF16), final_norm.reshape(1, d))
    return out.reshape(b, s, d)
```

```python
import math

import jax
import jax.numpy as jnp
import numpy as np
from jax import lax
from jax.experimental import pallas as pl
from jax.experimental.pallas import tpu as pltpu

D_MODEL = 1024
HEAD_DIM = 64
DIFF_WIDTH = D_MODEL // 2
N_DIFF_HEADS = DIFF_WIDTH // (2 * HEAD_DIM)
SB_WIDTH = D_MODEL - DIFF_WIDTH
N_SB_HEADS = SB_WIDTH // HEAD_DIM
N_SB_PAIRS = N_SB_HEADS // 2
MIX_WIDTH = DIFF_WIDTH + SB_WIDTH
D_FF = 4 * D_MODEL
EPS = 1e-6
LAYER_IDX = 0
LAM_INIT = 0.8 - 0.6 * math.exp(-0.3 * LAYER_IDX)
ATTN_SCALE = 1.0 / math.sqrt(HEAD_DIM)
LOG2E = math.log2(math.e)

LANES = 128
GROUPS = 4
ATTN_TILE = 256
TOKEN_TILE = 1024
MLP_TOKEN_TILE = 1024
FF_CHUNK = 1024
MASKED = -1e30
VMEM_LIMIT_BYTES = 56 * 1024 * 1024

_F32 = jnp.float32
_BF16 = jnp.bfloat16


def _rmsnorm(x, gain):
    inv = lax.rsqrt(jnp.mean(x * x, axis=-1, keepdims=True) + EPS)
    return x * inv * gain


def _dot(a, b):
    return jnp.dot(a, b, preferred_element_type=_F32)


def _dot_nt(a, b):
    return lax.dot_general(a, b, (((1,), (1,)), ((), ())), preferred_element_type=_F32)


def _lane_tile(x, reps):
    return jnp.concatenate([x] * reps, axis=1)


def _in_proj_kernel(x_ref, g_ref, w_ref, cs_ref, o_ref):
    a = _rmsnorm(x_ref[...], g_ref[...])
    res = (_dot(a.astype(_BF16), w_ref[...]) * cs_ref[...]).astype(_BF16)
    for g in range(o_ref.shape[0]):
        o_ref[g] = res[:, g * LANES:(g + 1) * LANES]


def _in_proj(x2d, gain, w_bf16, col_scale):
    n_tok = x2d.shape[0]
    n_out = w_bf16.shape[1]
    n_groups = n_out // LANES
    return pl.pallas_call(
        _in_proj_kernel,
        grid=(n_tok // TOKEN_TILE,),
        in_specs=[
            pl.BlockSpec((TOKEN_TILE, D_MODEL), lambda i: (i, 0)),
            pl.BlockSpec((1, D_MODEL), lambda i: (0, 0)),
            pl.BlockSpec((D_MODEL, n_out), lambda i: (0, 0)),
            pl.BlockSpec((1, n_out), lambda i: (0, 0)),
        ],
        out_specs=pl.BlockSpec((n_groups, TOKEN_TILE, LANES), lambda i: (0, i, 0)),
        out_shape=jax.ShapeDtypeStruct((n_groups, n_tok, LANES), _BF16),
        compiler_params=pltpu.CompilerParams(vmem_limit_bytes=VMEM_LIMIT_BYTES),
        name="in_proj",
    )(x2d, gain, w_bf16, col_scale)


def _split_halves(x_bf16):
    lane = lax.broadcasted_iota(jnp.int32, (1, LANES), 1)
    zero = jnp.zeros_like(x_bf16)
    return (jnp.where(lane < HEAD_DIM, x_bf16, zero), jnp.where(lane >= HEAD_DIM, x_bf16, zero))


def _rows(blk):
    start = blk * ATTN_TILE
    if not isinstance(start, int):
        start = pl.multiple_of(start, ATTN_TILE)
    return pl.ds(start, ATTN_TILE)


def _as_f32(i):
    return jnp.asarray(i, jnp.int32).astype(_F32)


def _group_spec(seq, section):
    return pl.BlockSpec((GROUPS, seq, LANES), lambda bi: (section, bi, 0))


def _offdiag_steps(n_blk):
    steps = [(qi, kj) for qi in range(1, n_blk) for kj in range(qi - 1, -1, -1)]
    return (jnp.asarray(np.array([qk[0] for qk in steps], np.int32)),
            jnp.asarray(np.array([qk[1] for qk in steps], np.int32)))


def _skewed_pipeline(n_units, stages):
    depth = len(stages)
    assert n_units >= depth

    def iteration(j, parity, lo, hi):
        for k in range(depth - 1, -1, -1):
            if lo <= k < hi:
                stages[k](j - k, (parity - k) % 2)

    for j in range(depth - 1):
        iteration(j, j % 2, 0, j + 1)

    start = depth - 1
    n_pairs, odd = divmod(n_units - start, 2)

    def body(i, c):
        j = start + 2 * i
        iteration(j, start % 2, 0, depth)
        iteration(j + 1, (start + 1) % 2, 0, depth)
        return c

    lax.fori_loop(0, n_pairs, body, 0)
    if odd:
        iteration(n_units - 1, (n_units - 1) % 2, 0, depth)
    for j in range(n_units, n_units + depth - 1):
        iteration(j, j % 2, j - n_units + 1, depth)


def _diff_attn_kernel(slopes_ref, lq1_ref, lk1_ref, lq2_ref, lk2_ref,
                      g_ref, q_ref, k_ref, v_ref, o_ref, m_ref, a_ref, dbias_ref):
    t = ATTN_TILE
    n_blk = q_ref.shape[1] // t
    n_h = q_ref.shape[0]

    lam = (jnp.exp(jnp.sum(lq1_ref[...] * lk1_ref[...], axis=-1, keepdims=True))
           - jnp.exp(jnp.sum(lq2_ref[...] * lk2_ref[...], axis=-1, keepdims=True))
           + LAM_INIT)

    assert n_blk % 2 == 0
    wide = 2 * t
    row = lax.broadcasted_iota(jnp.int32, (t, t), 0)
    col = lax.broadcasted_iota(jnp.int32, (t, t), 1)
    colf = lax.broadcasted_iota(jnp.int32, (1, wide), 1).astype(_F32)
    for h in range(n_h):
        diag = jnp.where(row >= col, slopes_ref[h] * col.astype(_F32), MASKED)
        dbias_ref[h, :, t:] = diag
        dbias_ref[h, :, :t] = jnp.broadcast_to(slopes_ref[h] * (colf[:, :t] - t), (t, t))

    def key_step(kind, qi, k0):
        first = kind != "full"
        width = t if kind == "diag" else wide
        ones = jnp.ones((width, LANES), _BF16)
        if not isinstance(k0, int):
            k0 = pl.multiple_of(k0, t)
        for head in range(n_h):
            q_halves = _split_halves(q_ref[head, _rows(qi), :])
            k = k_ref[head, pl.ds(k0, width), :]
            v_aug = jnp.concatenate([v_ref[head, pl.ds(k0, width), :], ones], axis=1)
            if kind == "diag":
                bias = dbias_ref[head, :, t:]
            elif kind == "pair_diag":
                bias = dbias_ref[head]
            else:
                bias = slopes_ref[head] * (colf + _as_f32(k0 - qi * t))
            for half in range(2):
                idx = 2 * head + half
                s = _dot_nt(q_halves[half], k) + bias
                m_curr = jnp.max(s, axis=1, keepdims=True)
                if first:
                    m_new = jnp.broadcast_to(m_curr, (t, LANES))
                else:
                    m_prev = m_ref[idx]
                    m_new = jnp.maximum(m_prev, m_curr)
                p = jnp.exp2(s - _lane_tile(m_new, width // LANES))
                pv = _dot(p.astype(_BF16), v_aug)
                if first:
                    a_ref[idx] = pv
                else:
                    a_ref[idx] = _lane_tile(jnp.exp2(m_prev - m_new), 2) * a_ref[idx] + pv
                m_ref[idx] = m_new

    def finish(qi):
        for h in range(n_h):
            a1 = a_ref[2 * h]
            a2 = a_ref[2 * h + 1]
            o = a1[:, :LANES] / a1[:, LANES:] - lam * (a2[:, :LANES] / a2[:, LANES:])
            o = _rmsnorm(o, g_ref[...]) * (1.0 - LAM_INIT)
            o_ref[h, _rows(qi), :] = o.astype(o_ref.dtype)

    def query_block_pair(r, carry):
        for parity, first_kind in ((0, "diag"), (1, "pair_diag")):
            qi = 2 * r + parity
            key_step(first_kind, qi, 2 * r * t)

            def full_chunk(c, cc, qi=qi):
                key_step("full", qi, c * wide)
                return cc

            lax.fori_loop(0, r, full_chunk, 0)
            finish(qi)
        return carry

    lax.fori_loop(0, n_blk // 2, query_block_pair, 0)


def _diff_attn(qkv_g, batch, slopes2, lq1, lk1, lq2, lk2, gain):
    n_tok = qkv_g.shape[1]
    s = n_tok // batch
    n_h = N_DIFF_HEADS
    vec = lambda n: pl.BlockSpec((1, n), lambda bi: (0, 0))
    return pl.pallas_call(
        _diff_attn_kernel,
        grid=(batch,),
        in_specs=[
            pl.BlockSpec(memory_space=pltpu.SMEM),
            vec(HEAD_DIM), vec(HEAD_DIM), vec(HEAD_DIM), vec(HEAD_DIM), vec(LANES),
            _group_spec(s, 0), _group_spec(s, 1), _group_spec(s, 2),
        ],
        out_specs=pl.BlockSpec((GROUPS, s, LANES), lambda bi: (0, bi, 0)),
        out_shape=jax.ShapeDtypeStruct((GROUPS, n_tok, LANES), _BF16),
        scratch_shapes=[pltpu.VMEM((2 * n_h, ATTN_TILE, LANES), _F32),
                        pltpu.VMEM((2 * n_h, ATTN_TILE, 2 * LANES), _F32),
                        pltpu.VMEM((n_h, ATTN_TILE, 2 * ATTN_TILE), _F32)],
        compiler_params=pltpu.CompilerParams(vmem_limit_bytes=VMEM_LIMIT_BYTES),
        name="diff_attn",
    )(slopes2, lq1, lk1, lq2, lk2, gain, qkv_g, qkv_g, qkv_g)


def _sb_attn_kernel(steps_q_ref, steps_k_ref, g_ref, q_ref, k_ref, v_ref, o_ref,
                    c_ref, acc_ref, z_ref, x_ref, lb_ref, w_ref, tri_ref, dmask_ref,
                    qm_ref, vm_ref):
    t = ATTN_TILE
    reps = t // LANES
    n_blk = q_ref.shape[1] // t
    n_g = q_ref.shape[0]
    g_shift = n_g.bit_length() - 1
    assert n_g == 1 << g_shift

    row = lax.broadcasted_iota(jnp.int32, (t, t), 0)
    col = lax.broadcasted_iota(jnp.int32, (t, t), 1)
    tri_ref[...] = jnp.where(row > col, -1.0, 0.0).astype(_BF16)
    dmask_ref[...] = jnp.where(col < row, 0.0, MASKED)

    def split_block(j, c):
        for grp in range(n_g):
            for src_ref, dst_ref in ((q_ref, qm_ref), (v_ref, vm_ref)):
                halves = _split_halves(src_ref[grp, _rows(j), :])
                dst_ref[grp, 0, _rows(j), :] = halves[0]
                dst_ref[grp, 1, _rows(j), :] = halves[1]
        return c

    lax.fori_loop(0, n_blk, split_block, 0)

    def make_stages(diagonal):
        bands = ((0, t // 2, t // 2), (t // 2, t // 2, t)) if diagonal else ((0, t, t),)

        def decode(u):
            step = u >> g_shift
            grp = u & (n_g - 1)
            if diagonal:
                return step, step, grp
            return steps_q_ref[step], steps_k_ref[step], grp

        def band_rows(blk, r0, n):
            start = blk * t + r0
            if not isinstance(start, int):
                start = pl.multiple_of(start, LANES)
            return pl.ds(start, n)

        def logits(u, slot):
            qi, kj, grp = decode(u)
            for r0, nr, nc in bands:
                k = k_ref[grp, band_rows(kj, 0, nc), :]
                for half in range(2):
                    z = _dot_nt(qm_ref[grp, half, band_rows(qi, r0, nr), :], k)
                    if diagonal:
                        z = z + dmask_ref[r0:r0 + nr, :nc]
                    z_ref[slot, half, r0:r0 + nr, :nc] = z

        def log_terms(u, slot):
            qi, kj, grp = decode(u)
            for r0, nr, nc in bands:
                for half in range(2):
                    head = 2 * grp + half
                    z = z_ref[slot, half, r0:r0 + nr, :nc]
                    sp = jnp.log(1.0 + jnp.exp2(-jnp.abs(z))) * LOG2E
                    neg_log_om = jnp.maximum(z, 0.0) + sp
                    log_beta = z - neg_log_om
                    om_sum = jnp.sum(neg_log_om, axis=1, keepdims=True)
                    x_ref[slot, half, r0:r0 + nr, :nc] = neg_log_om.astype(_BF16)
                    if diagonal:
                        lb_ref[slot, half, r0:r0 + nr, :nc] = log_beta
                        c_ref[qi, head, r0:r0 + nr, :] = jnp.broadcast_to(om_sum, (nr, LANES))
                    else:
                        carry = c_ref[qi, head]
                        lb_ref[slot, half] = log_beta - _lane_tile(carry, reps)
                        c_ref[qi, head] = carry + om_sum

        def weights(u, slot):
            for r0, nr, nc in bands:
                for half in range(2):
                    later = _dot(x_ref[slot, half, r0:r0 + nr, :nc], tri_ref[:nc, :nc])
                    e = later + lb_ref[slot, half, r0:r0 + nr, :nc]
                    w_ref[slot, half, r0:r0 + nr, :nc] = jnp.exp2(e).astype(_BF16)

        def values(u, slot):
            qi, kj, grp = decode(u)
            for r0, nr, nc in bands:
                pv = (_dot(w_ref[slot, 0, r0:r0 + nr, :nc], vm_ref[grp, 0, band_rows(kj, 0, nc), :])
                      + _dot(w_ref[slot, 1, r0:r0 + nr, :nc], vm_ref[grp, 1, band_rows(kj, 0, nc), :]))
                if diagonal:
                    acc_ref[qi, grp, r0:r0 + nr, :] = pv
                else:
                    acc_ref[qi, grp] += pv

        return logits, log_terms, weights, values

    _skewed_pipeline(n_blk * n_g, make_stages(True))
    _skewed_pipeline(steps_q_ref.shape[0] * n_g, make_stages(False))

    li = lax.broadcasted_iota(jnp.int32, (LANES, LANES), 0)
    lj = lax.broadcasted_iota(jnp.int32, (LANES, LANES), 1)
    head_mean = jnp.where((li < HEAD_DIM) == (lj < HEAD_DIM), 1.0 / HEAD_DIM, 0.0).astype(_BF16)

    def finish(qi, carry):
        for grp in range(n_g):
            o = acc_ref[qi, grp]
            ms = _dot((o * o).astype(_BF16), head_mean)
            o_ref[grp, _rows(qi), :] = (o * lax.rsqrt(ms + EPS) * g_ref[...]).astype(o_ref.dtype)
        return carry

    lax.fori_loop(0, n_blk, finish, 0)


def _sb_attn(qkv_g, batch, gain2):
    n_tok = qkv_g.shape[1]
    s = n_tok // batch
    n_blk = s // ATTN_TILE
    steps_q, steps_k = _offdiag_steps(n_blk)
    smem = pl.BlockSpec(memory_space=pltpu.SMEM)
    return pl.pallas_call(
        _sb_attn_kernel,
        grid=(batch,),
        in_specs=[
            smem, smem,
            pl.BlockSpec((1, LANES), lambda bi: (0, 0)),
            _group_spec(s, 3), _group_spec(s, 4), _group_spec(s, 5),
        ],
        out_specs=pl.BlockSpec((GROUPS, s, LANES), lambda bi: (0, bi, 0)),
        out_shape=jax.ShapeDtypeStruct((GROUPS, n_tok, LANES), _BF16),
        scratch_shapes=[pltpu.VMEM((n_blk, N_SB_HEADS, ATTN_TILE, LANES), _F32),
                        pltpu.VMEM((n_blk, GROUPS, ATTN_TILE, LANES), _F32),
                        pltpu.VMEM((2, 2, ATTN_TILE, ATTN_TILE), _F32),
                        pltpu.VMEM((2, 2, ATTN_TILE, ATTN_TILE), _BF16),
                        pltpu.VMEM((2, 2, ATTN_TILE, ATTN_TILE), _F32),
                        pltpu.VMEM((2, 2, ATTN_TILE, ATTN_TILE), _BF16),
                        pltpu.VMEM((ATTN_TILE, ATTN_TILE), _BF16),
                        pltpu.VMEM((ATTN_TILE, ATTN_TILE), _F32),
                        pltpu.VMEM((GROUPS, 2, s, LANES), _BF16),
                        pltpu.VMEM((GROUPS, 2, s, LANES), _BF16)],
        compiler_params=pltpu.CompilerParams(vmem_limit_bytes=VMEM_LIMIT_BYTES),
        name="sb_attn",
    )(steps_q, steps_k, gain2, qkv_g, qkv_g, qkv_g)


def _out_mlp_kernel(x_ref, md_ref, ms_ref, wo_ref, g2_ref, wup_ref, wdn_ref,
                    g3_ref, o_ref, m_ref, acc_ref):
    mixed = jnp.concatenate([md_ref[g] for g in range(md_ref.shape[0])]
                            + [ms_ref[g] for g in range(ms_ref.shape[0])], axis=1)
    h = x_ref[...] + _dot(mixed, wo_ref[...])
    m_ref[...] = _rmsnorm(h, g2_ref[...]).astype(_BF16)
    acc_ref[...] = h

    def ff_chunk(c, carry):
        cols = pl.ds(pl.multiple_of(c * FF_CHUNK, FF_CHUNK), FF_CHUNK)
        u = jnp.square(jnp.maximum(_dot(m_ref[...], wup_ref[:, cols]), 0.0))
        acc_ref[...] += _dot(u.astype(_BF16), wdn_ref[cols, :])
        return carry

    lax.fori_loop(0, D_FF // FF_CHUNK, ff_chunk, 0)
    o_ref[...] = _rmsnorm(acc_ref[...], g3_ref[...])


def _out_mlp(x2d, mix_d, mix_s, w_out, g2, w_up, w_dn, g3):
    n_tok = x2d.shape[0]
    const = lambda shape: pl.BlockSpec(shape, lambda i: (0, 0), pipeline_mode=pl.Buffered(1))
    tile = MLP_TOKEN_TILE
    mix = pl.BlockSpec((GROUPS, tile, LANES), lambda i: (0, i, 0))
    return pl.pallas_call(
        _out_mlp_kernel,
        grid=(n_tok // tile,),
        in_specs=[
            pl.BlockSpec((tile, D_MODEL), lambda i: (i, 0)),
            mix, mix,
            const((MIX_WIDTH, D_MODEL)),
            const((1, D_MODEL)),
            const((D_MODEL, D_FF)),
            const((D_FF, D_MODEL)),
            const((1, D_MODEL)),
        ],
        out_specs=pl.BlockSpec((tile, D_MODEL), lambda i: (i, 0)),
        out_shape=jax.ShapeDtypeStruct((n_tok, D_MODEL), _F32),
        scratch_shapes=[pltpu.VMEM((tile, D_MODEL), _BF16),
                        pltpu.VMEM((tile, D_MODEL), _F32)],
        compiler_params=pltpu.CompilerParams(vmem_limit_bytes=VMEM_LIMIT_BYTES),
        name="out_mlp",
    )(x2d, mix_d, mix_s, w_out, g2, w_up, w_dn, g3)


def _query_col_scale():
    qs = ATTN_SCALE * LOG2E
    one = jnp.ones((DIFF_WIDTH,), _F32)
    return jnp.concatenate([qs * one, one, one, qs * one, one, one]).reshape(1, 3 * MIX_WIDTH)


def kernel(x, attn_norm, w_in, lambda_q1, lambda_k1, lambda_q2, lambda_k2, diff_subln, sb_subln,
           w_out, mlp_norm, w_up, w_down, final_norm):
    assert attn_norm.shape[0] == 1, "single-layer block"
    b, s, d = x.shape
    x2d = x.reshape(b * s, d)

    qkv_g = _in_proj(x2d, attn_norm, w_in[0].astype(_BF16), _query_col_scale())

    slopes2 = jnp.asarray([LOG2E * 2.0 ** (-8.0 * (h + 1) / N_DIFF_HEADS)
                           for h in range(N_DIFF_HEADS)], dtype=_F32)
    mix_d = _diff_attn(qkv_g, b, slopes2, lambda_q1, lambda_k1, lambda_q2, lambda_k2, diff_subln)
    mix_s = _sb_attn(qkv_g, b, jnp.tile(sb_subln, (1, 2)))

    out = _out_mlp(x2d, mix_d, mix_s, w_out[0].astype(_BF16), mlp_norm,
                   w_up[0].astype(_BF16), w_down[0].astype(_BF16), final_norm.reshape(1, d))
    return out.reshape(b, s, d)
```

```python
import math

import jax
import jax.numpy as jnp
import numpy as np
from jax import lax
from jax.experimental import pallas as pl
from jax.experimental.pallas import tpu as pltpu

D_MODEL = 1024
HEAD_DIM = 64
DIFF_WIDTH = D_MODEL // 2
N_DIFF_HEADS = DIFF_WIDTH // (2 * HEAD_DIM)
SB_WIDTH = D_MODEL - DIFF_WIDTH
N_SB_HEADS = SB_WIDTH // HEAD_DIM
N_SB_PAIRS = N_SB_HEADS // 2
MIX_WIDTH = DIFF_WIDTH + SB_WIDTH
D_FF = 4 * D_MODEL
EPS = 1e-6
LAYER_IDX = 0
LAM_INIT = 0.8 - 0.6 * math.exp(-0.3 * LAYER_IDX)
ATTN_SCALE = 1.0 / math.sqrt(HEAD_DIM)
LOG2E = math.log2(math.e)

LANES = 128
GROUPS = 4
ATTN_TILE = 256
TOKEN_TILE = 1024
MLP_TOKEN_TILE = 1024
FF_CHUNK = 1024
MASKED = -1e30
VMEM_LIMIT_BYTES = 56 * 1024 * 1024

_F32 = jnp.float32
_BF16 = jnp.bfloat16


def _rmsnorm(x, gain):
    inv = lax.rsqrt(jnp.mean(x * x, axis=-1, keepdims=True) + EPS)
    return x * inv * gain


def _dot(a, b):
    return jnp.dot(a, b, preferred_element_type=_F32)


def _dot_nt(a, b):
    return lax.dot_general(a, b, (((1,), (1,)), ((), ())), preferred_element_type=_F32)


def _lane_tile(x, reps):
    return jnp.concatenate([x] * reps, axis=1)


def _in_proj_kernel(x_ref, g_ref, w_ref, cs_ref, o_ref):
    a = _rmsnorm(x_ref[...], g_ref[...])
    res = (_dot(a.astype(_BF16), w_ref[...]) * cs_ref[...]).astype(_BF16)
    for g in range(o_ref.shape[0]):
        o_ref[g] = res[:, g * LANES:(g + 1) * LANES]


def _in_proj(x2d, gain, w_bf16, col_scale):
    n_tok = x2d.shape[0]
    n_out = w_bf16.shape[1]
    n_groups = n_out // LANES
    return pl.pallas_call(
        _in_proj_kernel,
        grid=(n_tok // TOKEN_TILE,),
        in_specs=[
            pl.BlockSpec((TOKEN_TILE, D_MODEL), lambda i: (i, 0)),
            pl.BlockSpec((1, D_MODEL), lambda i: (0, 0)),
            pl.BlockSpec((D_MODEL, n_out), lambda i: (0, 0)),
            pl.BlockSpec((1, n_out), lambda i: (0, 0)),
        ],
        out_specs=pl.BlockSpec((n_groups, TOKEN_TILE, LANES), lambda i: (0, i, 0)),
        out_shape=jax.ShapeDtypeStruct((n_groups, n_tok, LANES), _BF16),
        compiler_params=pltpu.CompilerParams(vmem_limit_bytes=VMEM_LIMIT_BYTES),
        name="in_proj",
    )(x2d, gain, w_bf16, col_scale)


def _split_halves(x_bf16):
    lane = lax.broadcasted_iota(jnp.int32, (1, LANES), 1)
    zero = jnp.zeros_like(x_bf16)
    return (jnp.where(lane < HEAD_DIM, x_bf16, zero), jnp.where(lane >= HEAD_DIM, x_bf16, zero))


def _rows(blk):
    start = blk * ATTN_TILE
    if not isinstance(start, int):
        start = pl.multiple_of(start, ATTN_TILE)
    return pl.ds(start, ATTN_TILE)


def _as_f32(i):
    return jnp.asarray(i, jnp.int32).astype(_F32)


def _group_spec(seq, section):
    return pl.BlockSpec((GROUPS, seq, LANES), lambda bi: (section, bi, 0))


def _offdiag_steps(n_blk):
    steps = [(qi, kj) for qi in range(1, n_blk) for kj in range(qi - 1, -1, -1)]
    return (jnp.asarray(np.array([qk[0] for qk in steps], np.int32)),
            jnp.asarray(np.array([qk[1] for qk in steps], np.int32)))


def _skewed_pipeline(n_units, stages):
    depth = len(stages)
    assert n_units >= depth

    def iteration(j, parity, lo, hi):
        for k in range(depth - 1, -1, -1):
            if lo <= k < hi:
                stages[k](j - k, (parity - k) % 2)

    for j in range(depth - 1):
        iteration(j, j % 2, 0, j + 1)

    start = depth - 1
    n_pairs, odd = divmod(n_units - start, 2)

    def body(i, c):
        j = start + 2 * i
        iteration(j, start % 2, 0, depth)
        iteration(j + 1, (start + 1) % 2, 0, depth)
        return c

    lax.fori_loop(0, n_pairs, body, 0)
    if odd:
        iteration(n_units - 1, (n_units - 1) % 2, 0, depth)
    for j in range(n_units, n_units + depth - 1):
        iteration(j, j % 2, j - n_units + 1, depth)


def _diff_attn_kernel(slopes_ref, lq1_ref, lk1_ref, lq2_ref, lk2_ref,
                      g_ref, q_ref, k_ref, v_ref, o_ref, m_ref, a_ref, dbias_ref):
    t = ATTN_TILE
    n_blk = q_ref.shape[1] // t
    n_h = q_ref.shape[0]

    lam = (jnp.exp(jnp.sum(lq1_ref[...] * lk1_ref[...], axis=-1, keepdims=True))
           - jnp.exp(jnp.sum(lq2_ref[...] * lk2_ref[...], axis=-1, keepdims=True))
           + LAM_INIT)

    assert n_blk % 2 == 0
    wide = 2 * t
    row = lax.broadcasted_iota(jnp.int32, (t, t), 0)
    col = lax.broadcasted_iota(jnp.int32, (t, t), 1)
    colf = lax.broadcasted_iota(jnp.int32, (1, wide), 1).astype(_F32)
    for h in range(n_h):
        diag = jnp.where(row >= col, slopes_ref[h] * col.astype(_F32), MASKED)
        dbias_ref[h, :, t:] = diag
        dbias_ref[h, :, :t] = jnp.broadcast_to(slopes_ref[h] * (colf[:, :t] - t), (t, t))

    def key_step(kind, qi, k0, base):
        first = kind != "full"
        width = t if kind == "diag" else wide
        ones = jnp.ones((width, LANES), _BF16)
        if not isinstance(k0, int):
            k0 = pl.multiple_of(k0, t)
        for head in range(n_h):
            q_halves = _split_halves(q_ref[head, _rows(qi), :])
            k = k_ref[head, pl.ds(k0, width), :]
            v_aug = jnp.concatenate([v_ref[head, pl.ds(k0, width), :], ones], axis=1)
            if kind == "diag":
                bias = dbias_ref[head, :, t:]
            elif kind == "pair_diag":
                bias = dbias_ref[head]
            else:
                bias = slopes_ref[head] * (colf + _as_f32(k0 - qi * t))
            for half in range(2):
                idx = base + 2 * head + half
                s = _dot_nt(q_halves[half], k) + bias
                m_curr = jnp.max(s, axis=1, keepdims=True)
                if first:
                    m_new = jnp.broadcast_to(m_curr, (t, LANES))
                else:
                    m_prev = m_ref[idx]
                    m_new = jnp.maximum(m_prev, m_curr)
                p = jnp.exp2(s - _lane_tile(m_new, width // LANES))
                pv = _dot(p.astype(_BF16), v_aug)
                if first:
                    a_ref[idx] = pv
                else:
                    a_ref[idx] = _lane_tile(jnp.exp2(m_prev - m_new), 2) * a_ref[idx] + pv
                m_ref[idx] = m_new

    def finish(qi, base):
        for h in range(n_h):
            a1 = a_ref[base + 2 * h]
            a2 = a_ref[base + 2 * h + 1]
            o = a1[:, :LANES] / a1[:, LANES:] - lam * (a2[:, :LANES] / a2[:, LANES:])
            o = _rmsnorm(o, g_ref[...]) * (1.0 - LAM_INIT)
            o_ref[h, _rows(qi), :] = o.astype(o_ref.dtype)

    def query_block_pair(r, carry):
        key_step("diag", 2 * r, 2 * r * t, 0)
        key_step("pair_diag", 2 * r + 1, 2 * r * t, 2 * n_h)

        def full_chunk(c, cc):
            key_step("full", 2 * r, c * wide, 0)
            key_step("full", 2 * r + 1, c * wide, 2 * n_h)
            return cc

        lax.fori_loop(0, r, full_chunk, 0)
        finish(2 * r, 0)
        finish(2 * r + 1, 2 * n_h)
        return carry

    lax.fori_loop(0, n_blk // 2, query_block_pair, 0)


def _diff_attn(qkv_g, batch, slopes2, lq1, lk1, lq2, lk2, gain):
    n_tok = qkv_g.shape[1]
    s = n_tok // batch
    n_h = N_DIFF_HEADS
    vec = lambda n: pl.BlockSpec((1, n), lambda bi: (0, 0))
    return pl.pallas_call(
        _diff_attn_kernel,
        grid=(batch,),
        in_specs=[
            pl.BlockSpec(memory_space=pltpu.SMEM),
            vec(HEAD_DIM), vec(HEAD_DIM), vec(HEAD_DIM), vec(HEAD_DIM), vec(LANES),
            _group_spec(s, 0), _group_spec(s, 1), _group_spec(s, 2),
        ],
        out_specs=pl.BlockSpec((GROUPS, s, LANES), lambda bi: (0, bi, 0)),
        out_shape=jax.ShapeDtypeStruct((GROUPS, n_tok, LANES), _BF16),
        scratch_shapes=[pltpu.VMEM((4 * n_h, ATTN_TILE, LANES), _F32),
                        pltpu.VMEM((4 * n_h, ATTN_TILE, 2 * LANES), _F32),
                        pltpu.VMEM((n_h, ATTN_TILE, 2 * ATTN_TILE), _F32)],
        compiler_params=pltpu.CompilerParams(vmem_limit_bytes=VMEM_LIMIT_BYTES),
        name="diff_attn",
    )(slopes2, lq1, lk1, lq2, lk2, gain, qkv_g, qkv_g, qkv_g)


def _sb_attn_kernel(steps_q_ref, steps_k_ref, g_ref, q_ref, k_ref, v_ref, o_ref,
                    c_ref, acc_ref, z_ref, x_ref, lb_ref, w_ref, tri_ref, dmask_ref,
                    qm_ref, vm_ref):
    t = ATTN_TILE
    reps = t // LANES
    n_blk = q_ref.shape[1] // t
    n_g = q_ref.shape[0]
    g_shift = n_g.bit_length() - 1
    assert n_g == 1 << g_shift

    row = lax.broadcasted_iota(jnp.int32, (t, t), 0)
    col = lax.broadcasted_iota(jnp.int32, (t, t), 1)
    tri_ref[...] = jnp.where(row > col, -1.0, 0.0).astype(_BF16)
    dmask_ref[...] = jnp.where(col < row, 0.0, MASKED)

    def split_block(j, c):
        for grp in range(n_g):
            for src_ref, dst_ref in ((q_ref, qm_ref), (v_ref, vm_ref)):
                halves = _split_halves(src_ref[grp, _rows(j), :])
                dst_ref[grp, 0, _rows(j), :] = halves[0]
                dst_ref[grp, 1, _rows(j), :] = halves[1]
        return c

    lax.fori_loop(0, n_blk, split_block, 0)

    def make_stages(diagonal):
        bands = ((0, t // 2, t // 2), (t // 2, t // 2, t)) if diagonal else ((0, t, t),)

        def decode(u):
            step = u >> g_shift
            grp = u & (n_g - 1)
            if diagonal:
                return step, step, grp
            return steps_q_ref[step], steps_k_ref[step], grp

        def band_rows(blk, r0, n):
            start = blk * t + r0
            if not isinstance(start, int):
                start = pl.multiple_of(start, LANES)
            return pl.ds(start, n)

        def logits(u, slot):
            qi, kj, grp = decode(u)
            for r0, nr, nc in bands:
                k = k_ref[grp, band_rows(kj, 0, nc), :]
                for half in range(2):
                    z = _dot_nt(qm_ref[grp, half, band_rows(qi, r0, nr), :], k)
                    if diagonal:
                        z = z + dmask_ref[r0:r0 + nr, :nc]
                    z_ref[slot, half, r0:r0 + nr, :nc] = z

        def log_terms(u, slot):
            qi, kj, grp = decode(u)
            for r0, nr, nc in bands:
                for half in range(2):
                    head = 2 * grp + half
                    z = z_ref[slot, half, r0:r0 + nr, :nc]
                    sp = jnp.log(1.0 + jnp.exp2(-jnp.abs(z))) * LOG2E
                    neg_log_om = jnp.maximum(z, 0.0) + sp
                    log_beta = z - neg_log_om
                    om_sum = jnp.sum(neg_log_om, axis=1, keepdims=True)
                    x_ref[slot, half, r0:r0 + nr, :nc] = neg_log_om.astype(_BF16)
                    if diagonal:
                        lb_ref[slot, half, r0:r0 + nr, :nc] = log_beta
                        c_ref[qi, head, r0:r0 + nr, :] = jnp.broadcast_to(om_sum, (nr, LANES))
                    else:
                        carry = c_ref[qi, head]
                        lb_ref[slot, half] = log_beta - _lane_tile(carry, reps)
                        c_ref[qi, head] = carry + om_sum

        def weights(u, slot):
            for r0, nr, nc in bands:
                for half in range(2):
                    later = _dot(x_ref[slot, half, r0:r0 + nr, :nc], tri_ref[:nc, :nc])
                    e = later + lb_ref[slot, half, r0:r0 + nr, :nc]
                    w_ref[slot, half, r0:r0 + nr, :nc] = jnp.exp2(e).astype(_BF16)

        def values(u, slot):
            qi, kj, grp = decode(u)
            for r0, nr, nc in bands:
                pv = (_dot(w_ref[slot, 0, r0:r0 + nr, :nc], vm_ref[grp, 0, band_rows(kj, 0, nc), :])
                      + _dot(w_ref[slot, 1, r0:r0 + nr, :nc], vm_ref[grp, 1, band_rows(kj, 0, nc), :]))
                if diagonal:
                    acc_ref[qi, grp, r0:r0 + nr, :] = pv
                else:
                    acc_ref[qi, grp] += pv

        return logits, log_terms, weights, values

    _skewed_pipeline(n_blk * n_g, make_stages(True))
    _skewed_pipeline(steps_q_ref.shape[0] * n_g, make_stages(False))

    li = lax.broadcasted_iota(jnp.int32, (LANES, LANES), 0)
    lj = lax.broadcasted_iota(jnp.int32, (LANES, LANES), 1)
    head_mean = jnp.where((li < HEAD_DIM) == (lj < HEAD_DIM), 1.0 / HEAD_DIM, 0.0).astype(_BF16)

    def finish(qi, carry):
        for grp in range(n_g):
            o = acc_ref[qi, grp]
            ms = _dot((o * o).astype(_BF16), head_mean)
            o_ref[grp, _rows(qi), :] = (o * lax.rsqrt(ms + EPS) * g_ref[...]).astype(o_ref.dtype)
        return carry

    lax.fori_loop(0, n_blk, finish, 0)


def _sb_attn(qkv_g, batch, gain2):
    n_tok = qkv_g.shape[1]
    s = n_tok // batch
    n_blk = s // ATTN_TILE
    steps_q, steps_k = _offdiag_steps(n_blk)
    smem = pl.BlockSpec(memory_space=pltpu.SMEM)
    return pl.pallas_call(
        _sb_attn_kernel,
        grid=(batch,),
        in_specs=[
            smem, smem,
            pl.BlockSpec((1, LANES), lambda bi: (0, 0)),
            _group_spec(s, 3), _group_spec(s, 4), _group_spec(s, 5),
        ],
        out_specs=pl.BlockSpec((GROUPS, s, LANES), lambda bi: (0, bi, 0)),
        out_shape=jax.ShapeDtypeStruct((GROUPS, n_tok, LANES), _BF16),
        scratch_shapes=[pltpu.VMEM((n_blk, N_SB_HEADS, ATTN_TILE, LANES), _F32),
                        pltpu.VMEM((n_blk, GROUPS, ATTN_TILE, LANES), _F32),
                        pltpu.VMEM((2, 2, ATTN_TILE, ATTN_TILE), _F32),
                        pltpu.VMEM((2, 2, ATTN_TILE, ATTN_TILE), _BF16),
                        pltpu.VMEM((2, 2, ATTN_TILE, ATTN_TILE), _F32),
                        pltpu.VMEM((2, 2, ATTN_TILE, ATTN_TILE), _BF16),
                        pltpu.VMEM((ATTN_TILE, ATTN_TILE), _BF16),
                        pltpu.VMEM((ATTN_TILE, ATTN_TILE), _F32),
                        pltpu.VMEM((GROUPS, 2, s, LANES), _BF16),
                        pltpu.VMEM((GROUPS, 2, s, LANES), _BF16)],
        compiler_params=pltpu.CompilerParams(vmem_limit_bytes=VMEM_LIMIT_BYTES),
        name="sb_attn",
    )(steps_q, steps_k, gain2, qkv_g, qkv_g, qkv_g)


def _out_mlp_kernel(x_ref, md_ref, ms_ref, wo_ref, g2_ref, wup_ref, wdn_ref,
                    g3_ref, o_ref, m_ref, acc_ref):
    mixed = jnp.concatenate([md_ref[g] for g in range(md_ref.shape[0])]
                            + [ms_ref[g] for g in range(ms_ref.shape[0])], axis=1)
    h = x_ref[...] + _dot(mixed, wo_ref[...])
    m_ref[...] = _rmsnorm(h, g2_ref[...]).astype(_BF16)
    acc_ref[...] = h

    def ff_chunk(c, carry):
        cols = pl.ds(pl.multiple_of(c * FF_CHUNK, FF_CHUNK), FF_CHUNK)
        u = jnp.square(jnp.maximum(_dot(m_ref[...], wup_ref[:, cols]), 0.0))
        acc_ref[...] += _dot(u.astype(_BF16), wdn_ref[cols, :])
        return carry

    lax.fori_loop(0, D_FF // FF_CHUNK, ff_chunk, 0)
    o_ref[...] = _rmsnorm(acc_ref[...], g3_ref[...])


def _out_mlp(x2d, mix_d, mix_s, w_out, g2, w_up, w_dn, g3):
    n_tok = x2d.shape[0]
    const = lambda shape: pl.BlockSpec(shape, lambda i: (0, 0), pipeline_mode=pl.Buffered(1))
    tile = MLP_TOKEN_TILE
    mix = pl.BlockSpec((GROUPS, tile, LANES), lambda i: (0, i, 0))
    return pl.pallas_call(
        _out_mlp_kernel,
        grid=(n_tok // tile,),
        in_specs=[
            pl.BlockSpec((tile, D_MODEL), lambda i: (i, 0)),
            mix, mix,
            const((MIX_WIDTH, D_MODEL)),
            const((1, D_MODEL)),
            const((D_MODEL, D_FF)),
            const((D_FF, D_MODEL)),
            const((1, D_MODEL)),
        ],
        out_specs=pl.BlockSpec((tile, D_MODEL), lambda i: (i, 0)),
        out_shape=jax.ShapeDtypeStruct((n_tok, D_MODEL), _F32),
        scratch_shapes=[pltpu.VMEM((tile, D_MODEL), _BF16),
                        pltpu.VMEM((tile, D_MODEL), _F32)],
        compiler_params=pltpu.CompilerParams(vmem_limit_bytes=VMEM_LIMIT_BYTES),
        name="out_mlp",
    )(x2d, mix_d, mix_s, w_out, g2, w_up, w_dn, g3)


def _query_col_scale():
    qs = ATTN_SCALE * LOG2E
    one = jnp.ones((DIFF_WIDTH,), _F32)
    return jnp.concatenate([qs * one, one, one, qs * one, one, one]).reshape(1, 3 * MIX_WIDTH)


def kernel(x, attn_norm, w_in, lambda_q1, lambda_k1, lambda_q2, lambda_k2, diff_subln, sb_subln,
           w_out, mlp_norm, w_up, w_down, final_norm):
    assert attn_norm.shape[0] == 1, "single-layer block"
    b, s, d = x.shape
    x2d = x.reshape(b * s, d)

    qkv_g = _in_proj(x2d, attn_norm, w_in[0].astype(_BF16), _query_col_scale())

    slopes2 = jnp.asarray([LOG2E * 2.0 ** (-8.0 * (h + 1) / N_DIFF_HEADS)
                           for h in range(N_DIFF_HEADS)], dtype=_F32)
    mix_d = _diff_attn(qkv_g, b, slopes2, lambda_q1, lambda_k1, lambda_q2, lambda_k2, diff_subln)
    mix_s = _sb_attn(qkv_g, b, jnp.tile(sb_subln, (1, 2)))

    out = _out_mlp(x2d, mix_d, mix_s, w_out[0].astype(_BF16), mlp_norm,
                   w_up[0].astype(_BF16), w_down[0].astype(_BF16), final_norm.reshape(1, d))
    return out.reshape(b, s, d)
```

```python
import math

import jax
import jax.numpy as jnp
import numpy as np
from jax import lax
from jax.experimental import pallas as pl
from jax.experimental.pallas import tpu as pltpu

D_MODEL = 1024
HEAD_DIM = 64
DIFF_WIDTH = D_MODEL // 2
N_DIFF_HEADS = DIFF_WIDTH // (2 * HEAD_DIM)
SB_WIDTH = D_MODEL - DIFF_WIDTH
N_SB_HEADS = SB_WIDTH // HEAD_DIM
N_SB_PAIRS = N_SB_HEADS // 2
MIX_WIDTH = DIFF_WIDTH + SB_WIDTH
D_FF = 4 * D_MODEL
EPS = 1e-6
LAYER_IDX = 0
LAM_INIT = 0.8 - 0.6 * math.exp(-0.3 * LAYER_IDX)
ATTN_SCALE = 1.0 / math.sqrt(HEAD_DIM)
LOG2E = math.log2(math.e)

LANES = 128
GROUPS = 4
ATTN_TILE = 256
UNIT_GROUPS = 1
TOKEN_TILE = 1024
MLP_TOKEN_TILE = 1024
FF_CHUNK = 1024
MASKED = -1e30
VMEM_LIMIT_BYTES = 56 * 1024 * 1024

_F32 = jnp.float32
_BF16 = jnp.bfloat16


def _rmsnorm(x, gain):
    inv = lax.rsqrt(jnp.mean(x * x, axis=-1, keepdims=True) + EPS)
    return x * inv * gain


def _dot(a, b):
    return jnp.dot(a, b, preferred_element_type=_F32)


def _dot_nt(a, b):
    return lax.dot_general(a, b, (((1,), (1,)), ((), ())), preferred_element_type=_F32)


def _lane_tile(x, reps):
    return jnp.concatenate([x] * reps, axis=1)


def _in_proj_kernel(x_ref, g_ref, w_ref, cs_ref, o_ref):
    a = _rmsnorm(x_ref[...], g_ref[...])
    res = (_dot(a.astype(_BF16), w_ref[...]) * cs_ref[...]).astype(_BF16)
    for g in range(o_ref.shape[0]):
        o_ref[g] = res[:, g * LANES:(g + 1) * LANES]


def _in_proj(x2d, gain, w_bf16, col_scale):
    n_tok = x2d.shape[0]
    n_out = w_bf16.shape[1]
    n_groups = n_out // LANES
    return pl.pallas_call(
        _in_proj_kernel,
        grid=(n_tok // TOKEN_TILE,),
        in_specs=[
            pl.BlockSpec((TOKEN_TILE, D_MODEL), lambda i: (i, 0)),
            pl.BlockSpec((1, D_MODEL), lambda i: (0, 0)),
            pl.BlockSpec((D_MODEL, n_out), lambda i: (0, 0)),
            pl.BlockSpec((1, n_out), lambda i: (0, 0)),
        ],
        out_specs=pl.BlockSpec((n_groups, TOKEN_TILE, LANES), lambda i: (0, i, 0)),
        out_shape=jax.ShapeDtypeStruct((n_groups, n_tok, LANES), _BF16),
        compiler_params=pltpu.CompilerParams(vmem_limit_bytes=VMEM_LIMIT_BYTES),
        name="in_proj",
    )(x2d, gain, w_bf16, col_scale)


def _split_halves(x_bf16):
    lane = lax.broadcasted_iota(jnp.int32, (1, LANES), 1)
    zero = jnp.zeros_like(x_bf16)
    return (jnp.where(lane < HEAD_DIM, x_bf16, zero), jnp.where(lane >= HEAD_DIM, x_bf16, zero))


def _rows(blk):
    start = blk * ATTN_TILE
    if not isinstance(start, int):
        start = pl.multiple_of(start, ATTN_TILE)
    return pl.ds(start, ATTN_TILE)


def _as_f32(i):
    return jnp.asarray(i, jnp.int32).astype(_F32)


def _group_spec(seq, section):
    return pl.BlockSpec((GROUPS, seq, LANES), lambda bi: (section, bi, 0))


def _offdiag_steps(n_blk):
    steps = [(qi, kj) for qi in range(1, n_blk) for kj in range(qi - 1, -1, -1)]
    return (jnp.asarray(np.array([qk[0] for qk in steps], np.int32)),
            jnp.asarray(np.array([qk[1] for qk in steps], np.int32)))


def _skewed_pipeline(n_units, stages):
    depth = len(stages)
    assert n_units >= depth

    def iteration(j, parity, lo, hi):
        for k in range(depth - 1, -1, -1):
            if lo <= k < hi:
                stages[k](j - k, (parity - k) % 2)

    for j in range(depth - 1):
        iteration(j, j % 2, 0, j + 1)

    start = depth - 1
    n_pairs, odd = divmod(n_units - start, 2)

    def body(i, c):
        j = start + 2 * i
        iteration(j, start % 2, 0, depth)
        iteration(j + 1, (start + 1) % 2, 0, depth)
        return c

    lax.fori_loop(0, n_pairs, body, 0)
    if odd:
        iteration(n_units - 1, (n_units - 1) % 2, 0, depth)
    for j in range(n_units, n_units + depth - 1):
        iteration(j, j % 2, j - n_units + 1, depth)


def _diff_attn_kernel(slopes_ref, lq1_ref, lk1_ref, lq2_ref, lk2_ref,
                      g_ref, q_ref, k_ref, v_ref, o_ref, m_ref, a_ref, dbias_ref):
    t = ATTN_TILE
    n_blk = q_ref.shape[1] // t
    n_h = q_ref.shape[0]

    lam = (jnp.exp(jnp.sum(lq1_ref[...] * lk1_ref[...], axis=-1, keepdims=True))
           - jnp.exp(jnp.sum(lq2_ref[...] * lk2_ref[...], axis=-1, keepdims=True))
           + LAM_INIT)

    assert n_blk % 2 == 0
    wide = 2 * t
    row = lax.broadcasted_iota(jnp.int32, (t, t), 0)
    col = lax.broadcasted_iota(jnp.int32, (t, t), 1)
    colf = lax.broadcasted_iota(jnp.int32, (1, wide), 1).astype(_F32)
    for h in range(n_h):
        diag = jnp.where(row >= col, slopes_ref[h] * col.astype(_F32), MASKED)
        dbias_ref[h, :, t:] = diag
        dbias_ref[h, :, :t] = jnp.broadcast_to(slopes_ref[h] * (colf[:, :t] - t), (t, t))

    def key_step(kind, qi, k0, base):
        first = kind != "full"
        width = t if kind == "diag" else wide
        ones = jnp.ones((width, LANES), _BF16)
        if not isinstance(k0, int):
            k0 = pl.multiple_of(k0, t)
        for head in range(n_h):
            q_halves = _split_halves(q_ref[head, _rows(qi), :])
            k = k_ref[head, pl.ds(k0, width), :]
            v_aug = jnp.concatenate([v_ref[head, pl.ds(k0, width), :], ones], axis=1)
            if kind == "diag":
                bias = dbias_ref[head, :, t:]
            elif kind == "pair_diag":
                bias = dbias_ref[head]
            else:
                bias = slopes_ref[head] * (colf + _as_f32(k0 - qi * t))
            for half in range(2):
                idx = base + 2 * head + half
                s = _dot_nt(q_halves[half], k) + bias
                m_curr = jnp.max(s, axis=1, keepdims=True)
                if first:
                    m_new = jnp.broadcast_to(m_curr, (t, LANES))
                else:
                    m_prev = m_ref[idx]
                    m_new = jnp.maximum(m_prev, m_curr)
                p = jnp.exp2(s - _lane_tile(m_new, width // LANES))
                pv = _dot(p.astype(_BF16), v_aug)
                if first:
                    a_ref[idx] = pv
                else:
                    a_ref[idx] = _lane_tile(jnp.exp2(m_prev - m_new), 2) * a_ref[idx] + pv
                m_ref[idx] = m_new

    def finish(qi, base):
        for h in range(n_h):
            a1 = a_ref[base + 2 * h]
            a2 = a_ref[base + 2 * h + 1]
            o = a1[:, :LANES] / a1[:, LANES:] - lam * (a2[:, :LANES] / a2[:, LANES:])
            o = _rmsnorm(o, g_ref[...]) * (1.0 - LAM_INIT)
            o_ref[h, _rows(qi), :] = o.astype(o_ref.dtype)

    for r in range(n_blk // 2):
        base = (r % 2) * 4 * n_h
        key_step("diag", 2 * r, 2 * r * t, base)
        key_step("pair_diag", 2 * r + 1, 2 * r * t, base + 2 * n_h)
        if r > 0:
            prev = ((r - 1) % 2) * 4 * n_h
            finish(2 * r - 2, prev)
            finish(2 * r - 1, prev + 2 * n_h)

        def full_chunk(c, cc, r=r, base=base):
            key_step("full", 2 * r, c * wide, base)
            key_step("full", 2 * r + 1, c * wide, base + 2 * n_h)
            return cc

        lax.fori_loop(0, r, full_chunk, 0)
    last = ((n_blk // 2 - 1) % 2) * 4 * n_h
    finish(n_blk - 2, last)
    finish(n_blk - 1, last + 2 * n_h)


def _diff_attn(qkv_g, batch, slopes2, lq1, lk1, lq2, lk2, gain):
    n_tok = qkv_g.shape[1]
    s = n_tok // batch
    n_h = N_DIFF_HEADS
    vec = lambda n: pl.BlockSpec((1, n), lambda bi: (0, 0))
    return pl.pallas_call(
        _diff_attn_kernel,
        grid=(batch,),
        in_specs=[
            pl.BlockSpec(memory_space=pltpu.SMEM),
            vec(HEAD_DIM), vec(HEAD_DIM), vec(HEAD_DIM), vec(HEAD_DIM), vec(LANES),
            _group_spec(s, 0), _group_spec(s, 1), _group_spec(s, 2),
        ],
        out_specs=pl.BlockSpec((GROUPS, s, LANES), lambda bi: (0, bi, 0)),
        out_shape=jax.ShapeDtypeStruct((GROUPS, n_tok, LANES), _BF16),
        scratch_shapes=[pltpu.VMEM((8 * n_h, ATTN_TILE, LANES), _F32),
                        pltpu.VMEM((8 * n_h, ATTN_TILE, 2 * LANES), _F32),
                        pltpu.VMEM((n_h, ATTN_TILE, 2 * ATTN_TILE), _F32)],
        compiler_params=pltpu.CompilerParams(vmem_limit_bytes=VMEM_LIMIT_BYTES),
        name="diff_attn",
    )(slopes2, lq1, lk1, lq2, lk2, gain, qkv_g, qkv_g, qkv_g)


def _sb_attn_kernel(steps_q_ref, steps_k_ref, g_ref, q_ref, k_ref, v_ref, o_ref,
                    c_ref, acc_ref, z_ref, x_ref, lb_ref, w_ref, tri_ref, dmask_ref,
                    qm_ref, vm_ref):
    t = ATTN_TILE
    reps = t // LANES
    n_blk = q_ref.shape[1] // t
    n_g = q_ref.shape[0]
    g_shift = n_g.bit_length() - 1
    assert n_g == 1 << g_shift

    row = lax.broadcasted_iota(jnp.int32, (t, t), 0)
    col = lax.broadcasted_iota(jnp.int32, (t, t), 1)
    tri_ref[...] = jnp.where(row > col, -1.0, 0.0).astype(_BF16)
    dmask_ref[...] = jnp.where(col < row, 0.0, MASKED)

    def split_block(j, c):
        for grp in range(n_g):
            for src_ref, dst_ref in ((q_ref, qm_ref), (v_ref, vm_ref)):
                halves = _split_halves(src_ref[grp, _rows(j), :])
                dst_ref[grp, 0, _rows(j), :] = halves[0]
                dst_ref[grp, 1, _rows(j), :] = halves[1]
        return c

    lax.fori_loop(0, n_blk, split_block, 0)

    def make_stages(diagonal):
        bands = ((0, t // 2, t // 2), (t // 2, t // 2, t)) if diagonal else ((0, t, t),)

        def decode(u):
            step = u >> g_shift
            grp = u & (n_g - 1)
            if diagonal:
                return step, step, grp
            return steps_q_ref[step], steps_k_ref[step], grp

        def band_rows(blk, r0, n):
            start = blk * t + r0
            if not isinstance(start, int):
                start = pl.multiple_of(start, LANES)
            return pl.ds(start, n)

        def logits(u, slot, gi):
            qi, kj, grp = decode(u)
            for r0, nr, nc in bands:
                k = k_ref[grp, band_rows(kj, 0, nc), :]
                for half in range(2):
                    z = _dot_nt(qm_ref[grp, half, band_rows(qi, r0, nr), :], k)
                    if diagonal:
                        z = z + dmask_ref[r0:r0 + nr, :nc]
                    z_ref[slot, 2 * gi + half, r0:r0 + nr, :nc] = z

        def log_terms(u, slot, gi):
            qi, kj, grp = decode(u)
            for r0, nr, nc in bands:
                for half in range(2):
                    head = 2 * grp + half
                    hand = 2 * gi + half
                    z = z_ref[slot, hand, r0:r0 + nr, :nc]
                    sp = jnp.log(1.0 + jnp.exp2(-jnp.abs(z))) * LOG2E
                    neg_log_om = jnp.maximum(z, 0.0) + sp
                    log_beta = z - neg_log_om
                    om_sum = jnp.sum(neg_log_om, axis=1, keepdims=True)
                    x_ref[slot, hand, r0:r0 + nr, :nc] = neg_log_om.astype(_BF16)
                    if diagonal:
                        lb_ref[slot, hand, r0:r0 + nr, :nc] = log_beta
                        c_ref[qi, head, r0:r0 + nr, :] = jnp.broadcast_to(om_sum, (nr, LANES))
                    else:
                        carry = c_ref[qi, head]
                        lb_ref[slot, hand] = log_beta - _lane_tile(carry, reps)
                        c_ref[qi, head] = carry + om_sum

        def weights(u, slot, gi):
            for r0, nr, nc in bands:
                for hand in range(2 * gi, 2 * gi + 2):
                    later = _dot(x_ref[slot, hand, r0:r0 + nr, :nc], tri_ref[:nc, :nc])
                    e = later + lb_ref[slot, hand, r0:r0 + nr, :nc]
                    w_ref[slot, hand, r0:r0 + nr, :nc] = jnp.exp2(e).astype(_BF16)

        def values(u, slot, gi):
            qi, kj, grp = decode(u)
            for r0, nr, nc in bands:
                keys = band_rows(kj, 0, nc)
                pv = (_dot(w_ref[slot, 2 * gi, r0:r0 + nr, :nc], vm_ref[grp, 0, keys, :])
                      + _dot(w_ref[slot, 2 * gi + 1, r0:r0 + nr, :nc], vm_ref[grp, 1, keys, :]))
                if diagonal:
                    acc_ref[qi, grp, r0:r0 + nr, :] = pv
                else:
                    acc_ref[qi, grp] += pv

        def per_unit(stage):
            def run(u, slot):
                for gi in range(UNIT_GROUPS):
                    stage(u * UNIT_GROUPS + gi, slot, gi)
            return run

        return [per_unit(stage) for stage in (logits, log_terms, weights, values)]

    units_per_step = n_g // UNIT_GROUPS
    _skewed_pipeline(n_blk * units_per_step, make_stages(True))
    _skewed_pipeline(steps_q_ref.shape[0] * units_per_step, make_stages(False))

    li = lax.broadcasted_iota(jnp.int32, (LANES, LANES), 0)
    lj = lax.broadcasted_iota(jnp.int32, (LANES, LANES), 1)
    head_mean = jnp.where((li < HEAD_DIM) == (lj < HEAD_DIM), 1.0 / HEAD_DIM, 0.0).astype(_BF16)

    def finish(qi, carry):
        for grp in range(n_g):
            o = acc_ref[qi, grp]
            ms = _dot((o * o).astype(_BF16), head_mean)
            o_ref[grp, _rows(qi), :] = (o * lax.rsqrt(ms + EPS) * g_ref[...]).astype(o_ref.dtype)
        return carry

    lax.fori_loop(0, n_blk, finish, 0)


def _sb_attn(qkv_g, batch, gain2):
    n_tok = qkv_g.shape[1]
    s = n_tok // batch
    n_blk = s // ATTN_TILE
    steps_q, steps_k = _offdiag_steps(n_blk)
    smem = pl.BlockSpec(memory_space=pltpu.SMEM)
    handoff = lambda dtype: pltpu.VMEM((2, 2 * UNIT_GROUPS, ATTN_TILE, ATTN_TILE), dtype)
    return pl.pallas_call(
        _sb_attn_kernel,
        grid=(batch,),
        in_specs=[
            smem, smem,
            pl.BlockSpec((1, LANES), lambda bi: (0, 0)),
            _group_spec(s, 3), _group_spec(s, 4), _group_spec(s, 5),
        ],
        out_specs=pl.BlockSpec((GROUPS, s, LANES), lambda bi: (0, bi, 0)),
        out_shape=jax.ShapeDtypeStruct((GROUPS, n_tok, LANES), _BF16),
        scratch_shapes=[pltpu.VMEM((n_blk, N_SB_HEADS, ATTN_TILE, LANES), _F32),
                        pltpu.VMEM((n_blk, GROUPS, ATTN_TILE, LANES), _F32),
                        handoff(_F32), handoff(_BF16), handoff(_F32), handoff(_BF16),
                        pltpu.VMEM((ATTN_TILE, ATTN_TILE), _BF16),
                        pltpu.VMEM((ATTN_TILE, ATTN_TILE), _F32),
                        pltpu.VMEM((GROUPS, 2, s, LANES), _BF16),
                        pltpu.VMEM((GROUPS, 2, s, LANES), _BF16)],
        compiler_params=pltpu.CompilerParams(vmem_limit_bytes=VMEM_LIMIT_BYTES),
        name="sb_attn",
    )(steps_q, steps_k, gain2, qkv_g, qkv_g, qkv_g)


def _out_mlp_kernel(x_ref, md_ref, ms_ref, wo_ref, g2_ref, wup_ref, wdn_ref,
                    g3_ref, o_ref, m_ref, acc_ref):
    mixed = jnp.concatenate([md_ref[g] for g in range(md_ref.shape[0])]
                            + [ms_ref[g] for g in range(ms_ref.shape[0])], axis=1)
    h = x_ref[...] + _dot(mixed, wo_ref[...])
    m_ref[...] = _rmsnorm(h, g2_ref[...]).astype(_BF16)
    acc_ref[...] = h

    def ff_chunk(c, carry):
        cols = pl.ds(pl.multiple_of(c * FF_CHUNK, FF_CHUNK), FF_CHUNK)
        u = jnp.square(jnp.maximum(_dot(m_ref[...], wup_ref[:, cols]), 0.0))
        acc_ref[...] += _dot(u.astype(_BF16), wdn_ref[cols, :])
        return carry

    lax.fori_loop(0, D_FF // FF_CHUNK, ff_chunk, 0)
    o_ref[...] = _rmsnorm(acc_ref[...], g3_ref[...])


def _out_mlp(x2d, mix_d, mix_s, w_out, g2, w_up, w_dn, g3):
    n_tok = x2d.shape[0]
    const = lambda shape: pl.BlockSpec(shape, lambda i: (0, 0), pipeline_mode=pl.Buffered(1))
    tile = MLP_TOKEN_TILE
    mix = pl.BlockSpec((GROUPS, tile, LANES), lambda i: (0, i, 0))
    return pl.pallas_call(
        _out_mlp_kernel,
        grid=(n_tok // tile,),
        in_specs=[
            pl.BlockSpec((tile, D_MODEL), lambda i: (i, 0)),
            mix, mix,
            const((MIX_WIDTH, D_MODEL)),
            const((1, D_MODEL)),
            const((D_MODEL, D_FF)),
            const((D_FF, D_MODEL)),
            const((1, D_MODEL)),
        ],
        out_specs=pl.BlockSpec((tile, D_MODEL), lambda i: (i, 0)),
        out_shape=jax.ShapeDtypeStruct((n_tok, D_MODEL), _F32),
        scratch_shapes=[pltpu.VMEM((tile, D_MODEL), _BF16),
                        pltpu.VMEM((tile, D_MODEL), _F32)],
        compiler_params=pltpu.CompilerParams(vmem_limit_bytes=VMEM_LIMIT_BYTES),
        name="out_mlp",
    )(x2d, mix_d, mix_s, w_out, g2, w_up, w_dn, g3)


def _query_col_scale():
    qs = ATTN_SCALE * LOG2E
    one = jnp.ones((DIFF_WIDTH,), _F32)
    return jnp.concatenate([qs * one, one, one, qs * one, one, one]).reshape(1, 3 * MIX_WIDTH)


def kernel(x, attn_norm, w_in, lambda_q1, lambda_k1, lambda_q2, lambda_k2, diff_subln, sb_subln,
           w_out, mlp_norm, w_up, w_down, final_norm):
    assert attn_norm.shape[0] == 1, "single-layer block"
    b, s, d = x.shape
    x2d = x.reshape(b * s, d)

    qkv_g = _in_proj(x2d, attn_norm, w_in[0].astype(_BF16), _query_col_scale())

    slopes2 = jnp.asarray([LOG2E * 2.0 ** (-8.0 * (h + 1) / N_DIFF_HEADS)
                           for h in range(N_DIFF_HEADS)], dtype=_F32)
    mix_d = _diff_attn(qkv_g, b, slopes2, lambda_q1, lambda_k1, lambda_q2, lambda_k2, diff_subln)
    mix_s = _sb_attn(qkv_g, b, jnp.tile(sb_subln, (1, 2)))

    out = _out_mlp(x2d, mix_d, mix_s, w_out[0].astype(_BF16), mlp_norm,
                   w_up[0].astype(_BF16), w_down[0].astype(_BF16), final_norm.reshape(1, d))
    return out.reshape(b, s, d)
```

```python
import math

import jax
import jax.numpy as jnp
import numpy as np
from jax import lax
from jax.experimental import pallas as pl
from jax.experimental.pallas import tpu as pltpu

D_MODEL = 1024
HEAD_DIM = 64
DIFF_WIDTH = D_MODEL // 2
N_DIFF_HEADS = DIFF_WIDTH // (2 * HEAD_DIM)
SB_WIDTH = D_MODEL - DIFF_WIDTH
N_SB_HEADS = SB_WIDTH // HEAD_DIM
N_SB_PAIRS = N_SB_HEADS // 2
MIX_WIDTH = DIFF_WIDTH + SB_WIDTH
D_FF = 4 * D_MODEL
EPS = 1e-6
LAYER_IDX = 0
LAM_INIT = 0.8 - 0.6 * math.exp(-0.3 * LAYER_IDX)
ATTN_SCALE = 1.0 / math.sqrt(HEAD_DIM)
LOG2E = math.log2(math.e)

LANES = 128
GROUPS = 4
ATTN_TILE = 256
UNIT_GROUPS = 1
TOKEN_TILE = 1024
MLP_TOKEN_TILE = 1024
FF_CHUNK = 1024
MASKED = -1e30
VMEM_LIMIT_BYTES = 56 * 1024 * 1024

_F32 = jnp.float32
_BF16 = jnp.bfloat16


def _rmsnorm(x, gain):
    inv = lax.rsqrt(jnp.mean(x * x, axis=-1, keepdims=True) + EPS)
    return x * inv * gain


def _dot(a, b):
    return jnp.dot(a, b, preferred_element_type=_F32)


def _dot_nt(a, b):
    return lax.dot_general(a, b, (((1,), (1,)), ((), ())), preferred_element_type=_F32)


def _lane_tile(x, reps):
    return jnp.concatenate([x] * reps, axis=1)


def _in_proj_kernel(x_ref, g_ref, w_ref, cs_ref, o_ref):
    a = _rmsnorm(x_ref[...], g_ref[...])
    res = (_dot(a.astype(_BF16), w_ref[...]) * cs_ref[...]).astype(_BF16)
    for g in range(o_ref.shape[0]):
        o_ref[g] = res[:, g * LANES:(g + 1) * LANES]


def _in_proj(x2d, gain, w_bf16, col_scale):
    n_tok = x2d.shape[0]
    n_out = w_bf16.shape[1]
    n_groups = n_out // LANES
    return pl.pallas_call(
        _in_proj_kernel,
        grid=(n_tok // TOKEN_TILE,),
        in_specs=[
            pl.BlockSpec((TOKEN_TILE, D_MODEL), lambda i: (i, 0)),
            pl.BlockSpec((1, D_MODEL), lambda i: (0, 0)),
            pl.BlockSpec((D_MODEL, n_out), lambda i: (0, 0)),
            pl.BlockSpec((1, n_out), lambda i: (0, 0)),
        ],
        out_specs=pl.BlockSpec((n_groups, TOKEN_TILE, LANES), lambda i: (0, i, 0)),
        out_shape=jax.ShapeDtypeStruct((n_groups, n_tok, LANES), _BF16),
        compiler_params=pltpu.CompilerParams(vmem_limit_bytes=VMEM_LIMIT_BYTES),
        name="in_proj",
    )(x2d, gain, w_bf16, col_scale)


def _split_halves(x_bf16):
    lane = lax.broadcasted_iota(jnp.int32, (1, LANES), 1)
    zero = jnp.zeros_like(x_bf16)
    return (jnp.where(lane < HEAD_DIM, x_bf16, zero), jnp.where(lane >= HEAD_DIM, x_bf16, zero))


def _rows(blk):
    start = blk * ATTN_TILE
    if not isinstance(start, int):
        start = pl.multiple_of(start, ATTN_TILE)
    return pl.ds(start, ATTN_TILE)


def _as_f32(i):
    return jnp.asarray(i, jnp.int32).astype(_F32)


def _group_spec(seq, section):
    return pl.BlockSpec((GROUPS, seq, LANES), lambda bi: (section, bi, 0))


def _offdiag_steps(n_blk):
    steps = [(qi, kj) for qi in range(1, n_blk) for kj in range(qi - 1, -1, -1)]
    return (jnp.asarray(np.array([qk[0] for qk in steps], np.int32)),
            jnp.asarray(np.array([qk[1] for qk in steps], np.int32)))


def _skewed_pipeline(n_units, stages):
    depth = len(stages)
    assert n_units >= depth

    def iteration(j, parity, lo, hi):
        for k in range(depth - 1, -1, -1):
            if lo <= k < hi:
                stages[k](j - k, (parity - k) % 2)

    for j in range(depth - 1):
        iteration(j, j % 2, 0, j + 1)

    start = depth - 1
    n_pairs, odd = divmod(n_units - start, 2)

    def body(i, c):
        j = start + 2 * i
        iteration(j, start % 2, 0, depth)
        iteration(j + 1, (start + 1) % 2, 0, depth)
        return c

    lax.fori_loop(0, n_pairs, body, 0)
    if odd:
        iteration(n_units - 1, (n_units - 1) % 2, 0, depth)
    for j in range(n_units, n_units + depth - 1):
        iteration(j, j % 2, j - n_units + 1, depth)


def _diff_attn_kernel(slopes_ref, lq1_ref, lk1_ref, lq2_ref, lk2_ref,
                      g_ref, q_ref, k_ref, v_ref, o_ref, m_ref, a_ref, dbias_ref):
    t = ATTN_TILE
    n_blk = q_ref.shape[1] // t
    n_h = q_ref.shape[0]

    lam = (jnp.exp(jnp.sum(lq1_ref[...] * lk1_ref[...], axis=-1, keepdims=True))
           - jnp.exp(jnp.sum(lq2_ref[...] * lk2_ref[...], axis=-1, keepdims=True))
           + LAM_INIT)

    assert n_blk % 2 == 0
    wide = 2 * t
    row = lax.broadcasted_iota(jnp.int32, (t, t), 0)
    col = lax.broadcasted_iota(jnp.int32, (t, t), 1)
    colf = lax.broadcasted_iota(jnp.int32, (1, wide), 1).astype(_F32)
    for h in range(n_h):
        diag = jnp.where(row >= col, slopes_ref[h] * col.astype(_F32), MASKED)
        dbias_ref[h, :, t:] = diag
        dbias_ref[h, :, :t] = jnp.broadcast_to(slopes_ref[h] * (colf[:, :t] - t), (t, t))

    def key_step(kind, qi, k0, base):
        first = kind != "full"
        width = t if kind == "diag" else wide
        ones = jnp.ones((width, LANES), _BF16)
        if not isinstance(k0, int):
            k0 = pl.multiple_of(k0, t)
        for head in range(n_h):
            q_halves = _split_halves(q_ref[head, _rows(qi), :])
            k = k_ref[head, pl.ds(k0, width), :]
            v_aug = jnp.concatenate([v_ref[head, pl.ds(k0, width), :], ones], axis=1)
            if kind == "diag":
                bias = dbias_ref[head, :, t:]
            elif kind == "pair_diag":
                bias = dbias_ref[head]
            else:
                bias = slopes_ref[head] * (colf + _as_f32(k0 - qi * t))
            for half in range(2):
                idx = base + 2 * head + half
                s = _dot_nt(q_halves[half], k) + bias
                m_curr = jnp.max(s, axis=1, keepdims=True)
                if first:
                    m_new = jnp.broadcast_to(m_curr, (t, LANES))
                else:
                    m_prev = m_ref[idx]
                    m_new = jnp.maximum(m_prev, m_curr)
                p = jnp.exp2(s - _lane_tile(m_new, width // LANES))
                pv = _dot(p.astype(_BF16), v_aug)
                if first:
                    a_ref[idx] = pv
                else:
                    a_ref[idx] = _lane_tile(jnp.exp2(m_prev - m_new), 2) * a_ref[idx] + pv
                m_ref[idx] = m_new

    def finish(qi, base):
        for h in range(n_h):
            a1 = a_ref[base + 2 * h]
            a2 = a_ref[base + 2 * h + 1]
            o = a1[:, :LANES] / a1[:, LANES:] - lam * (a2[:, :LANES] / a2[:, LANES:])
            o = _rmsnorm(o, g_ref[...]) * (1.0 - LAM_INIT)
            o_ref[h, _rows(qi), :] = o.astype(o_ref.dtype)

    for r in range(n_blk // 2):
        base = (r % 2) * 4 * n_h
        key_step("diag", 2 * r, 2 * r * t, base)
        key_step("pair_diag", 2 * r + 1, 2 * r * t, base + 2 * n_h)
        if r > 0:
            prev = ((r - 1) % 2) * 4 * n_h
            finish(2 * r - 2, prev)
            finish(2 * r - 1, prev + 2 * n_h)

        for c in range(r):
            key_step("full", 2 * r, c * wide, base)
            key_step("full", 2 * r + 1, c * wide, base + 2 * n_h)
    last = ((n_blk // 2 - 1) % 2) * 4 * n_h
    finish(n_blk - 2, last)
    finish(n_blk - 1, last + 2 * n_h)


def _diff_attn(qkv_g, batch, slopes2, lq1, lk1, lq2, lk2, gain):
    n_tok = qkv_g.shape[1]
    s = n_tok // batch
    n_h = N_DIFF_HEADS
    vec = lambda n: pl.BlockSpec((1, n), lambda bi: (0, 0))
    return pl.pallas_call(
        _diff_attn_kernel,
        grid=(batch,),
        in_specs=[
            pl.BlockSpec(memory_space=pltpu.SMEM),
            vec(HEAD_DIM), vec(HEAD_DIM), vec(HEAD_DIM), vec(HEAD_DIM), vec(LANES),
            _group_spec(s, 0), _group_spec(s, 1), _group_spec(s, 2),
        ],
        out_specs=pl.BlockSpec((GROUPS, s, LANES), lambda bi: (0, bi, 0)),
        out_shape=jax.ShapeDtypeStruct((GROUPS, n_tok, LANES), _BF16),
        scratch_shapes=[pltpu.VMEM((8 * n_h, ATTN_TILE, LANES), _F32),
                        pltpu.VMEM((8 * n_h, ATTN_TILE, 2 * LANES), _F32),
                        pltpu.VMEM((n_h, ATTN_TILE, 2 * ATTN_TILE), _F32)],
        compiler_params=pltpu.CompilerParams(vmem_limit_bytes=VMEM_LIMIT_BYTES),
        name="diff_attn",
    )(slopes2, lq1, lk1, lq2, lk2, gain, qkv_g, qkv_g, qkv_g)


def _sb_attn_kernel(steps_q_ref, steps_k_ref, g_ref, q_ref, k_ref, v_ref, o_ref,
                    c_ref, acc_ref, z_ref, x_ref, lb_ref, w_ref, tri_ref, dmask_ref,
                    qm_ref, vm_ref):
    t = ATTN_TILE
    reps = t // LANES
    n_blk = q_ref.shape[1] // t
    n_g = q_ref.shape[0]
    g_shift = n_g.bit_length() - 1
    assert n_g == 1 << g_shift

    row = lax.broadcasted_iota(jnp.int32, (t, t), 0)
    col = lax.broadcasted_iota(jnp.int32, (t, t), 1)
    tri_ref[...] = jnp.where(row > col, -1.0, 0.0).astype(_BF16)
    dmask_ref[...] = jnp.where(col < row, 0.0, MASKED)

    def split_block(j, c):
        for grp in range(n_g):
            for src_ref, dst_ref in ((q_ref, qm_ref), (v_ref, vm_ref)):
                halves = _split_halves(src_ref[grp, _rows(j), :])
                dst_ref[grp, 0, _rows(j), :] = halves[0]
                dst_ref[grp, 1, _rows(j), :] = halves[1]
        return c

    lax.fori_loop(0, n_blk, split_block, 0)

    def make_stages(diagonal):
        bands = ((0, t // 2, t // 2), (t // 2, t // 2, t)) if diagonal else ((0, t, t),)

        def decode(u):
            step = u >> g_shift
            grp = u & (n_g - 1)
            if diagonal:
                return step, step, grp
            return steps_q_ref[step], steps_k_ref[step], grp

        def band_rows(blk, r0, n):
            start = blk * t + r0
            if not isinstance(start, int):
                start = pl.multiple_of(start, LANES)
            return pl.ds(start, n)

        def logits(u, slot, gi):
            qi, kj, grp = decode(u)
            for r0, nr, nc in bands:
                k = k_ref[grp, band_rows(kj, 0, nc), :]
                for half in range(2):
                    z = _dot_nt(qm_ref[grp, half, band_rows(qi, r0, nr), :], k)
                    if diagonal:
                        z = z + dmask_ref[r0:r0 + nr, :nc]
                    z_ref[slot, 2 * gi + half, r0:r0 + nr, :nc] = z

        def log_terms(u, slot, gi):
            qi, kj, grp = decode(u)
            for r0, nr, nc in bands:
                for half in range(2):
                    head = 2 * grp + half
                    hand = 2 * gi + half
                    z = z_ref[slot, hand, r0:r0 + nr, :nc]
                    sp = jnp.log(1.0 + jnp.exp2(-jnp.abs(z))) * LOG2E
                    neg_log_om = jnp.maximum(z, 0.0) + sp
                    log_beta = z - neg_log_om
                    om_sum = jnp.sum(neg_log_om, axis=1, keepdims=True)
                    x_ref[slot, hand, r0:r0 + nr, :nc] = neg_log_om.astype(_BF16)
                    if diagonal:
                        lb_ref[slot, hand, r0:r0 + nr, :nc] = log_beta
                        c_ref[qi, head, r0:r0 + nr, :] = jnp.broadcast_to(om_sum, (nr, LANES))
                    else:
                        carry = c_ref[qi, head]
                        lb_ref[slot, hand] = log_beta - _lane_tile(carry, reps)
                        c_ref[qi, head] = carry + om_sum

        def weights(u, slot, gi):
            for r0, nr, nc in bands:
                for hand in range(2 * gi, 2 * gi + 2):
                    later = _dot(x_ref[slot, hand, r0:r0 + nr, :nc], tri_ref[:nc, :nc])
                    e = later + lb_ref[slot, hand, r0:r0 + nr, :nc]
                    w_ref[slot, hand, r0:r0 + nr, :nc] = jnp.exp2(e).astype(_BF16)

        def values(u, slot, gi):
            qi, kj, grp = decode(u)
            for r0, nr, nc in bands:
                keys = band_rows(kj, 0, nc)
                pv = (_dot(w_ref[slot, 2 * gi, r0:r0 + nr, :nc], vm_ref[grp, 0, keys, :])
                      + _dot(w_ref[slot, 2 * gi + 1, r0:r0 + nr, :nc], vm_ref[grp, 1, keys, :]))
                if diagonal:
                    acc_ref[qi, grp, r0:r0 + nr, :] = pv
                else:
                    acc_ref[qi, grp] += pv

        def per_unit(stage):
            def run(u, slot):
                for gi in range(UNIT_GROUPS):
                    stage(u * UNIT_GROUPS + gi, slot, gi)
            return run

        return [per_unit(stage) for stage in (logits, log_terms, weights, values)]

    units_per_step = n_g // UNIT_GROUPS
    _skewed_pipeline(n_blk * units_per_step, make_stages(True))
    _skewed_pipeline(steps_q_ref.shape[0] * units_per_step, make_stages(False))

    li = lax.broadcasted_iota(jnp.int32, (LANES, LANES), 0)
    lj = lax.broadcasted_iota(jnp.int32, (LANES, LANES), 1)
    head_mean = jnp.where((li < HEAD_DIM) == (lj < HEAD_DIM), 1.0 / HEAD_DIM, 0.0).astype(_BF16)

    def finish(qi, carry):
        for grp in range(n_g):
            o = acc_ref[qi, grp]
            ms = _dot((o * o).astype(_BF16), head_mean)
            o_ref[grp, _rows(qi), :] = (o * lax.rsqrt(ms + EPS) * g_ref[...]).astype(o_ref.dtype)
        return carry

    lax.fori_loop(0, n_blk, finish, 0)


def _sb_attn(qkv_g, batch, gain2):
    n_tok = qkv_g.shape[1]
    s = n_tok // batch
    n_blk = s // ATTN_TILE
    steps_q, steps_k = _offdiag_steps(n_blk)
    smem = pl.BlockSpec(memory_space=pltpu.SMEM)
    handoff = lambda dtype: pltpu.VMEM((2, 2 * UNIT_GROUPS, ATTN_TILE, ATTN_TILE), dtype)
    return pl.pallas_call(
        _sb_attn_kernel,
        grid=(batch,),
        in_specs=[
            smem, smem,
            pl.BlockSpec((1, LANES), lambda bi: (0, 0)),
            _group_spec(s, 3), _group_spec(s, 4), _group_spec(s, 5),
        ],
        out_specs=pl.BlockSpec((GROUPS, s, LANES), lambda bi: (0, bi, 0)),
        out_shape=jax.ShapeDtypeStruct((GROUPS, n_tok, LANES), _BF16),
        scratch_shapes=[pltpu.VMEM((n_blk, N_SB_HEADS, ATTN_TILE, LANES), _F32),
                        pltpu.VMEM((n_blk, GROUPS, ATTN_TILE, LANES), _F32),
                        handoff(_F32), handoff(_BF16), handoff(_F32), handoff(_BF16),
                        pltpu.VMEM((ATTN_TILE, ATTN_TILE), _BF16),
                        pltpu.VMEM((ATTN_TILE, ATTN_TILE), _F32),
                        pltpu.VMEM((GROUPS, 2, s, LANES), _BF16),
                        pltpu.VMEM((GROUPS, 2, s, LANES), _BF16)],
        compiler_params=pltpu.CompilerParams(vmem_limit_bytes=VMEM_LIMIT_BYTES),
        name="sb_attn",
    )(steps_q, steps_k, gain2, qkv_g, qkv_g, qkv_g)


def _out_mlp_kernel(x_ref, md_ref, ms_ref, wo_ref, g2_ref, wup_ref, wdn_ref,
                    g3_ref, o_ref, m_ref, acc_ref):
    mixed = jnp.concatenate([md_ref[g] for g in range(md_ref.shape[0])]
                            + [ms_ref[g] for g in range(ms_ref.shape[0])], axis=1)
    h = x_ref[...] + _dot(mixed, wo_ref[...])
    m_ref[...] = _rmsnorm(h, g2_ref[...]).astype(_BF16)
    acc_ref[...] = h

    def ff_chunk(c, carry):
        cols = pl.ds(pl.multiple_of(c * FF_CHUNK, FF_CHUNK), FF_CHUNK)
        u = jnp.square(jnp.maximum(_dot(m_ref[...], wup_ref[:, cols]), 0.0))
        acc_ref[...] += _dot(u.astype(_BF16), wdn_ref[cols, :])
        return carry

    lax.fori_loop(0, D_FF // FF_CHUNK, ff_chunk, 0)
    o_ref[...] = _rmsnorm(acc_ref[...], g3_ref[...])


def _out_mlp(x2d, mix_d, mix_s, w_out, g2, w_up, w_dn, g3):
    n_tok = x2d.shape[0]
    const = lambda shape: pl.BlockSpec(shape, lambda i: (0, 0), pipeline_mode=pl.Buffered(1))
    tile = MLP_TOKEN_TILE
    mix = pl.BlockSpec((GROUPS, tile, LANES), lambda i: (0, i, 0))
    return pl.pallas_call(
        _out_mlp_kernel,
        grid=(n_tok // tile,),
        in_specs=[
            pl.BlockSpec((tile, D_MODEL), lambda i: (i, 0)),
            mix, mix,
            const((MIX_WIDTH, D_MODEL)),
            const((1, D_MODEL)),
            const((D_MODEL, D_FF)),
            const((D_FF, D_MODEL)),
            const((1, D_MODEL)),
        ],
        out_specs=pl.BlockSpec((tile, D_MODEL), lambda i: (i, 0)),
        out_shape=jax.ShapeDtypeStruct((n_tok, D_MODEL), _F32),
        scratch_shapes=[pltpu.VMEM((tile, D_MODEL), _BF16),
                        pltpu.VMEM((tile, D_MODEL), _F32)],
        compiler_params=pltpu.CompilerParams(vmem_limit_bytes=VMEM_LIMIT_BYTES),
        name="out_mlp",
    )(x2d, mix_d, mix_s, w_out, g2, w_up, w_dn, g3)


def _query_col_scale():
    qs = ATTN_SCALE * LOG2E
    one = jnp.ones((DIFF_WIDTH,), _F32)
    return jnp.concatenate([qs * one, one, one, qs * one, one, one]).reshape(1, 3 * MIX_WIDTH)


def kernel(x, attn_norm, w_in, lambda_q1, lambda_k1, lambda_q2, lambda_k2, diff_subln, sb_subln,
           w_out, mlp_norm, w_up, w_down, final_norm):
    assert attn_norm.shape[0] == 1, "single-layer block"
    b, s, d = x.shape
    x2d = x.reshape(b * s, d)

    qkv_g = _in_proj(x2d, attn_norm, w_in[0].astype(_BF16), _query_col_scale())

    slopes2 = jnp.asarray([LOG2E * 2.0 ** (-8.0 * (h + 1) / N_DIFF_HEADS)
                           for h in range(N_DIFF_HEADS)], dtype=_F32)
    mix_d = _diff_attn(qkv_g, b, slopes2, lambda_q1, lambda_k1, lambda_q2, lambda_k2, diff_subln)
    mix_s = _sb_attn(qkv_g, b, jnp.tile(sb_subln, (1, 2)))

    out = _out_mlp(x2d, mix_d, mix_s, w_out[0].astype(_BF16), mlp_norm,
                   w_up[0].astype(_BF16), w_down[0].astype(_BF16), final_norm.reshape(1, d))
    return out.reshape(b, s, d)
```

```python
import math

import jax
import jax.numpy as jnp
import numpy as np
from jax import lax
from jax.experimental import pallas as pl
from jax.experimental.pallas import tpu as pltpu

D_MODEL = 1024
HEAD_DIM = 64
DIFF_WIDTH = D_MODEL // 2
N_DIFF_HEADS = DIFF_WIDTH // (2 * HEAD_DIM)
SB_WIDTH = D_MODEL - DIFF_WIDTH
N_SB_HEADS = SB_WIDTH // HEAD_DIM
N_SB_PAIRS = N_SB_HEADS // 2
MIX_WIDTH = DIFF_WIDTH + SB_WIDTH
D_FF = 4 * D_MODEL
EPS = 1e-6
LAYER_IDX = 0
LAM_INIT = 0.8 - 0.6 * math.exp(-0.3 * LAYER_IDX)
ATTN_SCALE = 1.0 / math.sqrt(HEAD_DIM)
LOG2E = math.log2(math.e)

LANES = 128
GROUPS = 4
ATTN_TILE = 256
UNIT_GROUPS = 1
PIPELINE_UNROLL = 4
TOKEN_TILE = 1024
MLP_TOKEN_TILE = 1024
FF_CHUNK = 1024
MASKED = -1e30
VMEM_LIMIT_BYTES = 56 * 1024 * 1024

_F32 = jnp.float32
_BF16 = jnp.bfloat16


def _rmsnorm(x, gain):
    inv = lax.rsqrt(jnp.mean(x * x, axis=-1, keepdims=True) + EPS)
    return x * inv * gain


def _dot(a, b):
    return jnp.dot(a, b, preferred_element_type=_F32)


def _dot_nt(a, b):
    return lax.dot_general(a, b, (((1,), (1,)), ((), ())), preferred_element_type=_F32)


def _lane_tile(x, reps):
    return jnp.concatenate([x] * reps, axis=1)


def _in_proj_kernel(x_ref, g_ref, w_ref, cs_ref, o_ref):
    a = _rmsnorm(x_ref[...], g_ref[...])
    res = (_dot(a.astype(_BF16), w_ref[...]) * cs_ref[...]).astype(_BF16)
    for g in range(o_ref.shape[0]):
        o_ref[g] = res[:, g * LANES:(g + 1) * LANES]


def _in_proj(x2d, gain, w_bf16, col_scale):
    n_tok = x2d.shape[0]
    n_out = w_bf16.shape[1]
    n_groups = n_out // LANES
    return pl.pallas_call(
        _in_proj_kernel,
        grid=(n_tok // TOKEN_TILE,),
        in_specs=[
            pl.BlockSpec((TOKEN_TILE, D_MODEL), lambda i: (i, 0)),
            pl.BlockSpec((1, D_MODEL), lambda i: (0, 0)),
            pl.BlockSpec((D_MODEL, n_out), lambda i: (0, 0)),
            pl.BlockSpec((1, n_out), lambda i: (0, 0)),
        ],
        out_specs=pl.BlockSpec((n_groups, TOKEN_TILE, LANES), lambda i: (0, i, 0)),
        out_shape=jax.ShapeDtypeStruct((n_groups, n_tok, LANES), _BF16),
        compiler_params=pltpu.CompilerParams(vmem_limit_bytes=VMEM_LIMIT_BYTES),
        name="in_proj",
    )(x2d, gain, w_bf16, col_scale)


def _split_halves(x_bf16):
    lane = lax.broadcasted_iota(jnp.int32, (1, LANES), 1)
    zero = jnp.zeros_like(x_bf16)
    return (jnp.where(lane < HEAD_DIM, x_bf16, zero), jnp.where(lane >= HEAD_DIM, x_bf16, zero))


def _rows(blk):
    start = blk * ATTN_TILE
    if not isinstance(start, int):
        start = pl.multiple_of(start, ATTN_TILE)
    return pl.ds(start, ATTN_TILE)


def _as_f32(i):
    return jnp.asarray(i, jnp.int32).astype(_F32)


def _group_spec(seq, section):
    return pl.BlockSpec((GROUPS, seq, LANES), lambda bi: (section, bi, 0))


def _offdiag_steps(n_blk):
    steps = [(qi, kj) for qi in range(1, n_blk) for kj in range(qi - 1, -1, -1)]
    return (jnp.asarray(np.array([qk[0] for qk in steps], np.int32)),
            jnp.asarray(np.array([qk[1] for qk in steps], np.int32)))


def _skewed_pipeline(n_units, stages):
    depth = len(stages)
    assert n_units >= depth

    def iteration(j, parity, lo, hi):
        for k in range(depth - 1, -1, -1):
            if lo <= k < hi:
                stages[k](j - k, (parity - k) % 2)

    for j in range(depth - 1):
        iteration(j, j % 2, 0, j + 1)

    start = depth - 1
    n_trips, left = divmod(n_units - start, PIPELINE_UNROLL)

    def body(i, c):
        for k in range(PIPELINE_UNROLL):
            iteration(start + PIPELINE_UNROLL * i + k, (start + k) % 2, 0, depth)
        return c

    lax.fori_loop(0, n_trips, body, 0)
    for j in range(n_units - left, n_units):
        iteration(j, j % 2, 0, depth)
    for j in range(n_units, n_units + depth - 1):
        iteration(j, j % 2, j - n_units + 1, depth)


def _diff_attn_kernel(slopes_ref, lq1_ref, lk1_ref, lq2_ref, lk2_ref,
                      g_ref, q_ref, k_ref, v_ref, o_ref, m_ref, a_ref, dbias_ref):
    t = ATTN_TILE
    n_blk = q_ref.shape[1] // t
    n_h = q_ref.shape[0]

    lam = (jnp.exp(jnp.sum(lq1_ref[...] * lk1_ref[...], axis=-1, keepdims=True))
           - jnp.exp(jnp.sum(lq2_ref[...] * lk2_ref[...], axis=-1, keepdims=True))
           + LAM_INIT)

    assert n_blk % 2 == 0
    wide = 2 * t
    row = lax.broadcasted_iota(jnp.int32, (t, t), 0)
    col = lax.broadcasted_iota(jnp.int32, (t, t), 1)
    colf = lax.broadcasted_iota(jnp.int32, (1, wide), 1).astype(_F32)
    for h in range(n_h):
        diag = jnp.where(row >= col, slopes_ref[h] * col.astype(_F32), MASKED)
        dbias_ref[h, :, t:] = diag
        dbias_ref[h, :, :t] = jnp.broadcast_to(slopes_ref[h] * (colf[:, :t] - t), (t, t))

    def key_step(kind, qi, k0, base):
        first = kind != "full"
        width = t if kind == "diag" else wide
        ones = jnp.ones((width, LANES), _BF16)
        if not isinstance(k0, int):
            k0 = pl.multiple_of(k0, t)
        for head in range(n_h):
            q_halves = _split_halves(q_ref[head, _rows(qi), :])
            k = k_ref[head, pl.ds(k0, width), :]
            v_aug = jnp.concatenate([v_ref[head, pl.ds(k0, width), :], ones], axis=1)
            if kind == "diag":
                bias = dbias_ref[head, :, t:]
            elif kind == "pair_diag":
                bias = dbias_ref[head]
            else:
                bias = slopes_ref[head] * (colf + _as_f32(k0 - qi * t))
            for half in range(2):
                idx = base + 2 * head + half
                s = _dot_nt(q_halves[half], k) + bias
                m_curr = jnp.max(s, axis=1, keepdims=True)
                if first:
                    m_new = jnp.broadcast_to(m_curr, (t, LANES))
                else:
                    m_prev = m_ref[idx]
                    m_new = jnp.maximum(m_prev, m_curr)
                p = jnp.exp2(s - _lane_tile(m_new, width // LANES))
                pv = _dot(p.astype(_BF16), v_aug)
                if first:
                    a_ref[idx] = pv
                else:
                    a_ref[idx] = _lane_tile(jnp.exp2(m_prev - m_new), 2) * a_ref[idx] + pv
                m_ref[idx] = m_new

    def finish(qi, base):
        for h in range(n_h):
            a1 = a_ref[base + 2 * h]
            a2 = a_ref[base + 2 * h + 1]
            o = a1[:, :LANES] / a1[:, LANES:] - lam * (a2[:, :LANES] / a2[:, LANES:])
            o = _rmsnorm(o, g_ref[...]) * (1.0 - LAM_INIT)
            o_ref[h, _rows(qi), :] = o.astype(o_ref.dtype)

    for r in range(n_blk // 2):
        base = (r % 2) * 4 * n_h
        key_step("diag", 2 * r, 2 * r * t, base)
        key_step("pair_diag", 2 * r + 1, 2 * r * t, base + 2 * n_h)
        if r > 0:
            prev = ((r - 1) % 2) * 4 * n_h
            finish(2 * r - 2, prev)
            finish(2 * r - 1, prev + 2 * n_h)

        for c in range(r):
            key_step("full", 2 * r, c * wide, base)
            key_step("full", 2 * r + 1, c * wide, base + 2 * n_h)
    last = ((n_blk // 2 - 1) % 2) * 4 * n_h
    finish(n_blk - 2, last)
    finish(n_blk - 1, last + 2 * n_h)


def _diff_attn(qkv_g, batch, slopes2, lq1, lk1, lq2, lk2, gain):
    n_tok = qkv_g.shape[1]
    s = n_tok // batch
    n_h = N_DIFF_HEADS
    vec = lambda n: pl.BlockSpec((1, n), lambda bi: (0, 0))
    return pl.pallas_call(
        _diff_attn_kernel,
        grid=(batch,),
        in_specs=[
            pl.BlockSpec(memory_space=pltpu.SMEM),
            vec(HEAD_DIM), vec(HEAD_DIM), vec(HEAD_DIM), vec(HEAD_DIM), vec(LANES),
            _group_spec(s, 0), _group_spec(s, 1), _group_spec(s, 2),
        ],
        out_specs=pl.BlockSpec((GROUPS, s, LANES), lambda bi: (0, bi, 0)),
        out_shape=jax.ShapeDtypeStruct((GROUPS, n_tok, LANES), _BF16),
        scratch_shapes=[pltpu.VMEM((8 * n_h, ATTN_TILE, LANES), _F32),
                        pltpu.VMEM((8 * n_h, ATTN_TILE, 2 * LANES), _F32),
                        pltpu.VMEM((n_h, ATTN_TILE, 2 * ATTN_TILE), _F32)],
        compiler_params=pltpu.CompilerParams(vmem_limit_bytes=VMEM_LIMIT_BYTES),
        name="diff_attn",
    )(slopes2, lq1, lk1, lq2, lk2, gain, qkv_g, qkv_g, qkv_g)


def _sb_attn_kernel(steps_q_ref, steps_k_ref, g_ref, q_ref, k_ref, v_ref, o_ref,
                    c_ref, acc_ref, z_ref, x_ref, lb_ref, w_ref, tri_ref, dmask_ref,
                    qm_ref, vm_ref):
    t = ATTN_TILE
    reps = t // LANES
    n_blk = q_ref.shape[1] // t
    n_g = q_ref.shape[0]
    g_shift = n_g.bit_length() - 1
    assert n_g == 1 << g_shift

    row = lax.broadcasted_iota(jnp.int32, (t, t), 0)
    col = lax.broadcasted_iota(jnp.int32, (t, t), 1)
    tri_ref[...] = jnp.where(row > col, -1.0, 0.0).astype(_BF16)
    dmask_ref[...] = jnp.where(col < row, 0.0, MASKED)

    def split_block(j, c):
        for grp in range(n_g):
            for src_ref, dst_ref in ((q_ref, qm_ref), (v_ref, vm_ref)):
                halves = _split_halves(src_ref[grp, _rows(j), :])
                dst_ref[grp, 0, _rows(j), :] = halves[0]
                dst_ref[grp, 1, _rows(j), :] = halves[1]
        return c

    lax.fori_loop(0, n_blk, split_block, 0)

    def make_stages(diagonal):
        bands = ((0, t // 2, t // 2), (t // 2, t // 2, t)) if diagonal else ((0, t, t),)

        def decode(u):
            step = u >> g_shift
            grp = u & (n_g - 1)
            if diagonal:
                return step, step, grp
            return steps_q_ref[step], steps_k_ref[step], grp

        def band_rows(blk, r0, n):
            start = blk * t + r0
            if not isinstance(start, int):
                start = pl.multiple_of(start, LANES)
            return pl.ds(start, n)

        def logits(u, slot, gi):
            qi, kj, grp = decode(u)
            for r0, nr, nc in bands:
                k = k_ref[grp, band_rows(kj, 0, nc), :]
                for half in range(2):
                    z = _dot_nt(qm_ref[grp, half, band_rows(qi, r0, nr), :], k)
                    if diagonal:
                        z = z + dmask_ref[r0:r0 + nr, :nc]
                    z_ref[slot, 2 * gi + half, r0:r0 + nr, :nc] = z

        def log_terms(u, slot, gi):
            qi, kj, grp = decode(u)
            for r0, nr, nc in bands:
                for half in range(2):
                    head = 2 * grp + half
                    hand = 2 * gi + half
                    z = z_ref[slot, hand, r0:r0 + nr, :nc]
                    sp = jnp.log(1.0 + jnp.exp2(-jnp.abs(z))) * LOG2E
                    neg_log_om = jnp.maximum(z, 0.0) + sp
                    log_beta = z - neg_log_om
                    om_sum = jnp.sum(neg_log_om, axis=1, keepdims=True)
                    x_ref[slot, hand, r0:r0 + nr, :nc] = neg_log_om.astype(_BF16)
                    if diagonal:
                        lb_ref[slot, hand, r0:r0 + nr, :nc] = log_beta
                        c_ref[qi, head, r0:r0 + nr, :] = jnp.broadcast_to(om_sum, (nr, LANES))
                    else:
                        carry = c_ref[qi, head]
                        lb_ref[slot, hand] = log_beta - _lane_tile(carry, reps)
                        c_ref[qi, head] = carry + om_sum

        def weights(u, slot, gi):
            for r0, nr, nc in bands:
                for hand in range(2 * gi, 2 * gi + 2):
                    later = _dot(x_ref[slot, hand, r0:r0 + nr, :nc], tri_ref[:nc, :nc])
                    e = later + lb_ref[slot, hand, r0:r0 + nr, :nc]
                    w_ref[slot, hand, r0:r0 + nr, :nc] = jnp.exp2(e).astype(_BF16)

        def values(u, slot, gi):
            qi, kj, grp = decode(u)
            for r0, nr, nc in bands:
                keys = band_rows(kj, 0, nc)
                pv = (_dot(w_ref[slot, 2 * gi, r0:r0 + nr, :nc], vm_ref[grp, 0, keys, :])
                      + _dot(w_ref[slot, 2 * gi + 1, r0:r0 + nr, :nc], vm_ref[grp, 1, keys, :]))
                if diagonal:
                    acc_ref[qi, grp, r0:r0 + nr, :] = pv
                else:
                    acc_ref[qi, grp] += pv

        def per_unit(stage):
            def run(u, slot):
                for gi in range(UNIT_GROUPS):
                    stage(u * UNIT_GROUPS + gi, slot, gi)
            return run

        return [per_unit(stage) for stage in (logits, log_terms, weights, values)]

    units_per_step = n_g // UNIT_GROUPS
    _skewed_pipeline(n_blk * units_per_step, make_stages(True))
    _skewed_pipeline(steps_q_ref.shape[0] * units_per_step, make_stages(False))

    li = lax.broadcasted_iota(jnp.int32, (LANES, LANES), 0)
    lj = lax.broadcasted_iota(jnp.int32, (LANES, LANES), 1)
    head_mean = jnp.where((li < HEAD_DIM) == (lj < HEAD_DIM), 1.0 / HEAD_DIM, 0.0).astype(_BF16)

    def finish(qi, carry):
        for grp in range(n_g):
            o = acc_ref[qi, grp]
            ms = _dot((o * o).astype(_BF16), head_mean)
            o_ref[grp, _rows(qi), :] = (o * lax.rsqrt(ms + EPS) * g_ref[...]).astype(o_ref.dtype)
        return carry

    lax.fori_loop(0, n_blk, finish, 0)


def _sb_attn(qkv_g, batch, gain2):
    n_tok = qkv_g.shape[1]
    s = n_tok // batch
    n_blk = s // ATTN_TILE
    steps_q, steps_k = _offdiag_steps(n_blk)
    smem = pl.BlockSpec(memory_space=pltpu.SMEM)
    handoff = lambda dtype: pltpu.VMEM((2, 2 * UNIT_GROUPS, ATTN_TILE, ATTN_TILE), dtype)
    return pl.pallas_call(
        _sb_attn_kernel,
        grid=(batch,),
        in_specs=[
            smem, smem,
            pl.BlockSpec((1, LANES), lambda bi: (0, 0)),
            _group_spec(s, 3), _group_spec(s, 4), _group_spec(s, 5),
        ],
        out_specs=pl.BlockSpec((GROUPS, s, LANES), lambda bi: (0, bi, 0)),
        out_shape=jax.ShapeDtypeStruct((GROUPS, n_tok, LANES), _BF16),
        scratch_shapes=[pltpu.VMEM((n_blk, N_SB_HEADS, ATTN_TILE, LANES), _F32),
                        pltpu.VMEM((n_blk, GROUPS, ATTN_TILE, LANES), _F32),
                        handoff(_F32), handoff(_BF16), handoff(_F32), handoff(_BF16),
                        pltpu.VMEM((ATTN_TILE, ATTN_TILE), _BF16),
                        pltpu.VMEM((ATTN_TILE, ATTN_TILE), _F32),
                        pltpu.VMEM((GROUPS, 2, s, LANES), _BF16),
                        pltpu.VMEM((GROUPS, 2, s, LANES), _BF16)],
        compiler_params=pltpu.CompilerParams(vmem_limit_bytes=VMEM_LIMIT_BYTES),
        name="sb_attn",
    )(steps_q, steps_k, gain2, qkv_g, qkv_g, qkv_g)


def _out_mlp_kernel(x_ref, md_ref, ms_ref, wo_ref, g2_ref, wup_ref, wdn_ref,
                    g3_ref, o_ref, m_ref, acc_ref):
    mixed = jnp.concatenate([md_ref[g] for g in range(md_ref.shape[0])]
                            + [ms_ref[g] for g in range(ms_ref.shape[0])], axis=1)
    h = x_ref[...] + _dot(mixed, wo_ref[...])
    m_ref[...] = _rmsnorm(h, g2_ref[...]).astype(_BF16)
    acc_ref[...] = h

    def ff_chunk(c, carry):
        cols = pl.ds(pl.multiple_of(c * FF_CHUNK, FF_CHUNK), FF_CHUNK)
        u = jnp.square(jnp.maximum(_dot(m_ref[...], wup_ref[:, cols]), 0.0))
        acc_ref[...] += _dot(u.astype(_BF16), wdn_ref[cols, :])
        return carry

    lax.fori_loop(0, D_FF // FF_CHUNK, ff_chunk, 0)
    o_ref[...] = _rmsnorm(acc_ref[...], g3_ref[...])


def _out_mlp(x2d, mix_d, mix_s, w_out, g2, w_up, w_dn, g3):
    n_tok = x2d.shape[0]
    const = lambda shape: pl.BlockSpec(shape, lambda i: (0, 0), pipeline_mode=pl.Buffered(1))
    tile = MLP_TOKEN_TILE
    mix = pl.BlockSpec((GROUPS, tile, LANES), lambda i: (0, i, 0))
    return pl.pallas_call(
        _out_mlp_kernel,
        grid=(n_tok // tile,),
        in_specs=[
            pl.BlockSpec((tile, D_MODEL), lambda i: (i, 0)),
            mix, mix,
            const((MIX_WIDTH, D_MODEL)),
            const((1, D_MODEL)),
            const((D_MODEL, D_FF)),
            const((D_FF, D_MODEL)),
            const((1, D_MODEL)),
        ],
        out_specs=pl.BlockSpec((tile, D_MODEL), lambda i: (i, 0)),
        out_shape=jax.ShapeDtypeStruct((n_tok, D_MODEL), _F32),
        scratch_shapes=[pltpu.VMEM((tile, D_MODEL), _BF16),
                        pltpu.VMEM((tile, D_MODEL), _F32)],
        compiler_params=pltpu.CompilerParams(vmem_limit_bytes=VMEM_LIMIT_BYTES),
        name="out_mlp",
    )(x2d, mix_d, mix_s, w_out, g2, w_up, w_dn, g3)


def _query_col_scale():
    qs = ATTN_SCALE * LOG2E
    one = jnp.ones((DIFF_WIDTH,), _F32)
    return jnp.concatenate([qs * one, one, one, qs * one, one, one]).reshape(1, 3 * MIX_WIDTH)


def kernel(x, attn_norm, w_in, lambda_q1, lambda_k1, lambda_q2, lambda_k2, diff_subln, sb_subln,
           w_out, mlp_norm, w_up, w_down, final_norm):
    assert attn_norm.shape[0] == 1, "single-layer block"
    b, s, d = x.shape
    x2d = x.reshape(b * s, d)

    qkv_g = _in_proj(x2d, attn_norm, w_in[0].astype(_BF16), _query_col_scale())

    slopes2 = jnp.asarray([LOG2E * 2.0 ** (-8.0 * (h + 1) / N_DIFF_HEADS)
                           for h in range(N_DIFF_HEADS)], dtype=_F32)
    mix_d = _diff_attn(qkv_g, b, slopes2, lambda_q1, lambda_k1, lambda_q2, lambda_k2, diff_subln)
    mix_s = _sb_attn(qkv_g, b, jnp.tile(sb_subln, (1, 2)))

    out = _out_mlp(x2d, mix_d, mix_s, w_out[0].astype(_BF16), mlp_norm,
                   w_up[0].astype(_BF16), w_down[0].astype(_BF16), final_norm.reshape(1, d))
    return out.reshape(b, s, d)
```

```python
import math

import jax
import jax.numpy as jnp
import numpy as np
from jax import lax
from jax.experimental import pallas as pl
from jax.experimental.pallas import tpu as pltpu

D_MODEL = 1024
HEAD_DIM = 64
DIFF_WIDTH = D_MODEL // 2
N_DIFF_HEADS = DIFF_WIDTH // (2 * HEAD_DIM)
SB_WIDTH = D_MODEL - DIFF_WIDTH
N_SB_HEADS = SB_WIDTH // HEAD_DIM
N_SB_PAIRS = N_SB_HEADS // 2
MIX_WIDTH = DIFF_WIDTH + SB_WIDTH
D_FF = 4 * D_MODEL
EPS = 1e-6
LAYER_IDX = 0
LAM_INIT = 0.8 - 0.6 * math.exp(-0.3 * LAYER_IDX)
ATTN_SCALE = 1.0 / math.sqrt(HEAD_DIM)
LOG2E = math.log2(math.e)

LANES = 128
GROUPS = 4
ATTN_TILE = 256
UNIT_GROUPS = 1
PIPELINE_UNROLL = 8
TOKEN_TILE = 1024
MLP_TOKEN_TILE = 1024
FF_CHUNK = 1024
MASKED = -1e30
VMEM_LIMIT_BYTES = 56 * 1024 * 1024

_F32 = jnp.float32
_BF16 = jnp.bfloat16


def _rmsnorm(x, gain):
    inv = lax.rsqrt(jnp.mean(x * x, axis=-1, keepdims=True) + EPS)
    return x * inv * gain


def _dot(a, b):
    return jnp.dot(a, b, preferred_element_type=_F32)


def _dot_nt(a, b):
    return lax.dot_general(a, b, (((1,), (1,)), ((), ())), preferred_element_type=_F32)


def _lane_tile(x, reps):
    return jnp.concatenate([x] * reps, axis=1)


def _in_proj_kernel(x_ref, g_ref, w_ref, cs_ref, o_ref):
    a = _rmsnorm(x_ref[...], g_ref[...])
    res = (_dot(a.astype(_BF16), w_ref[...]) * cs_ref[...]).astype(_BF16)
    for g in range(o_ref.shape[0]):
        o_ref[g] = res[:, g * LANES:(g + 1) * LANES]


def _in_proj(x2d, gain, w_bf16, col_scale):
    n_tok = x2d.shape[0]
    n_out = w_bf16.shape[1]
    n_groups = n_out // LANES
    return pl.pallas_call(
        _in_proj_kernel,
        grid=(n_tok // TOKEN_TILE,),
        in_specs=[
            pl.BlockSpec((TOKEN_TILE, D_MODEL), lambda i: (i, 0)),
            pl.BlockSpec((1, D_MODEL), lambda i: (0, 0)),
            pl.BlockSpec((D_MODEL, n_out), lambda i: (0, 0)),
            pl.BlockSpec((1, n_out), lambda i: (0, 0)),
        ],
        out_specs=pl.BlockSpec((n_groups, TOKEN_TILE, LANES), lambda i: (0, i, 0)),
        out_shape=jax.ShapeDtypeStruct((n_groups, n_tok, LANES), _BF16),
        compiler_params=pltpu.CompilerParams(vmem_limit_bytes=VMEM_LIMIT_BYTES),
        name="in_proj",
    )(x2d, gain, w_bf16, col_scale)


def _split_halves(x_bf16):
    lane = lax.broadcasted_iota(jnp.int32, (1, LANES), 1)
    zero = jnp.zeros_like(x_bf16)
    return (jnp.where(lane < HEAD_DIM, x_bf16, zero), jnp.where(lane >= HEAD_DIM, x_bf16, zero))


def _rows(blk):
    start = blk * ATTN_TILE
    if not isinstance(start, int):
        start = pl.multiple_of(start, ATTN_TILE)
    return pl.ds(start, ATTN_TILE)


def _as_f32(i):
    return jnp.asarray(i, jnp.int32).astype(_F32)


def _group_spec(seq, section):
    return pl.BlockSpec((GROUPS, seq, LANES), lambda bi: (section, bi, 0))


def _offdiag_steps(n_blk):
    steps = [(qi, kj) for qi in range(1, n_blk) for kj in range(qi - 1, -1, -1)]
    return (jnp.asarray(np.array([qk[0] for qk in steps], np.int32)),
            jnp.asarray(np.array([qk[1] for qk in steps], np.int32)))


def _skewed_pipeline(n_units, stages):
    depth = len(stages)
    assert n_units >= depth

    def iteration(j, parity, lo, hi):
        for k in range(depth - 1, -1, -1):
            if lo <= k < hi:
                stages[k](j - k, (parity - k) % 2)

    for j in range(depth - 1):
        iteration(j, j % 2, 0, j + 1)

    start = depth - 1
    n_trips, left = divmod(n_units - start, PIPELINE_UNROLL)

    def body(i, c):
        for k in range(PIPELINE_UNROLL):
            iteration(start + PIPELINE_UNROLL * i + k, (start + k) % 2, 0, depth)
        return c

    lax.fori_loop(0, n_trips, body, 0)
    for j in range(n_units - left, n_units):
        iteration(j, j % 2, 0, depth)
    for j in range(n_units, n_units + depth - 1):
        iteration(j, j % 2, j - n_units + 1, depth)


def _diff_attn_kernel(slopes_ref, lq1_ref, lk1_ref, lq2_ref, lk2_ref,
                      g_ref, q_ref, k_ref, v_ref, o_ref, m_ref, a_ref, dbias_ref):
    t = ATTN_TILE
    n_blk = q_ref.shape[1] // t
    n_h = q_ref.shape[0]

    lam = (jnp.exp(jnp.sum(lq1_ref[...] * lk1_ref[...], axis=-1, keepdims=True))
           - jnp.exp(jnp.sum(lq2_ref[...] * lk2_ref[...], axis=-1, keepdims=True))
           + LAM_INIT)

    assert n_blk % 2 == 0
    wide = 2 * t
    row = lax.broadcasted_iota(jnp.int32, (t, t), 0)
    col = lax.broadcasted_iota(jnp.int32, (t, t), 1)
    colf = lax.broadcasted_iota(jnp.int32, (1, wide), 1).astype(_F32)
    for h in range(n_h):
        diag = jnp.where(row >= col, slopes_ref[h] * col.astype(_F32), MASKED)
        dbias_ref[h, :, t:] = diag
        dbias_ref[h, :, :t] = jnp.broadcast_to(slopes_ref[h] * (colf[:, :t] - t), (t, t))

    def key_step(kind, qi, k0, base):
        first = kind != "full"
        width = t if kind == "diag" else wide
        ones = jnp.ones((width, LANES), _BF16)
        if not isinstance(k0, int):
            k0 = pl.multiple_of(k0, t)
        for head in range(n_h):
            q_halves = _split_halves(q_ref[head, _rows(qi), :])
            k = k_ref[head, pl.ds(k0, width), :]
            v_aug = jnp.concatenate([v_ref[head, pl.ds(k0, width), :], ones], axis=1)
            if kind == "diag":
                bias = dbias_ref[head, :, t:]
            elif kind == "pair_diag":
                bias = dbias_ref[head]
            else:
                bias = slopes_ref[head] * (colf + _as_f32(k0 - qi * t))
            for half in range(2):
                idx = base + 2 * head + half
                s = _dot_nt(q_halves[half], k) + bias
                m_curr = jnp.max(s, axis=1, keepdims=True)
                if first:
                    m_new = jnp.broadcast_to(m_curr, (t, LANES))
                else:
                    m_prev = m_ref[idx]
                    m_new = jnp.maximum(m_prev, m_curr)
                p = jnp.exp2(s - _lane_tile(m_new, width // LANES))
                pv = _dot(p.astype(_BF16), v_aug)
                if first:
                    a_ref[idx] = pv
                else:
                    a_ref[idx] = _lane_tile(jnp.exp2(m_prev - m_new), 2) * a_ref[idx] + pv
                m_ref[idx] = m_new

    def finish(qi, base):
        for h in range(n_h):
            a1 = a_ref[base + 2 * h]
            a2 = a_ref[base + 2 * h + 1]
            o = a1[:, :LANES] / a1[:, LANES:] - lam * (a2[:, :LANES] / a2[:, LANES:])
            o = _rmsnorm(o, g_ref[...]) * (1.0 - LAM_INIT)
            o_ref[h, _rows(qi), :] = o.astype(o_ref.dtype)

    for r in range(n_blk // 2):
        base = (r % 2) * 4 * n_h
        key_step("diag", 2 * r, 2 * r * t, base)
        key_step("pair_diag", 2 * r + 1, 2 * r * t, base + 2 * n_h)
        if r > 0:
            prev = ((r - 1) % 2) * 4 * n_h
            finish(2 * r - 2, prev)
            finish(2 * r - 1, prev + 2 * n_h)

        for c in range(r):
            key_step("full", 2 * r, c * wide, base)
            key_step("full", 2 * r + 1, c * wide, base + 2 * n_h)
    last = ((n_blk // 2 - 1) % 2) * 4 * n_h
    finish(n_blk - 2, last)
    finish(n_blk - 1, last + 2 * n_h)


def _diff_attn(qkv_g, batch, slopes2, lq1, lk1, lq2, lk2, gain):
    n_tok = qkv_g.shape[1]
    s = n_tok // batch
    n_h = N_DIFF_HEADS
    vec = lambda n: pl.BlockSpec((1, n), lambda bi: (0, 0))
    return pl.pallas_call(
        _diff_attn_kernel,
        grid=(batch,),
        in_specs=[
            pl.BlockSpec(memory_space=pltpu.SMEM),
            vec(HEAD_DIM), vec(HEAD_DIM), vec(HEAD_DIM), vec(HEAD_DIM), vec(LANES),
            _group_spec(s, 0), _group_spec(s, 1), _group_spec(s, 2),
        ],
        out_specs=pl.BlockSpec((GROUPS, s, LANES), lambda bi: (0, bi, 0)),
        out_shape=jax.ShapeDtypeStruct((GROUPS, n_tok, LANES), _BF16),
        scratch_shapes=[pltpu.VMEM((8 * n_h, ATTN_TILE, LANES), _F32),
                        pltpu.VMEM((8 * n_h, ATTN_TILE, 2 * LANES), _F32),
                        pltpu.VMEM((n_h, ATTN_TILE, 2 * ATTN_TILE), _F32)],
        compiler_params=pltpu.CompilerParams(vmem_limit_bytes=VMEM_LIMIT_BYTES),
        name="diff_attn",
    )(slopes2, lq1, lk1, lq2, lk2, gain, qkv_g, qkv_g, qkv_g)


def _sb_attn_kernel(steps_q_ref, steps_k_ref, g_ref, q_ref, k_ref, v_ref, o_ref,
                    c_ref, acc_ref, z_ref, x_ref, lb_ref, w_ref, tri_ref, dmask_ref,
                    qm_ref, vm_ref):
    t = ATTN_TILE
    reps = t // LANES
    n_blk = q_ref.shape[1] // t
    n_g = q_ref.shape[0]
    g_shift = n_g.bit_length() - 1
    assert n_g == 1 << g_shift

    row = lax.broadcasted_iota(jnp.int32, (t, t), 0)
    col = lax.broadcasted_iota(jnp.int32, (t, t), 1)
    tri_ref[...] = jnp.where(row > col, -1.0, 0.0).astype(_BF16)
    dmask_ref[...] = jnp.where(col < row, 0.0, MASKED)

    def split_block(j, c):
        for grp in range(n_g):
            for src_ref, dst_ref in ((q_ref, qm_ref), (v_ref, vm_ref)):
                halves = _split_halves(src_ref[grp, _rows(j), :])
                dst_ref[grp, 0, _rows(j), :] = halves[0]
                dst_ref[grp, 1, _rows(j), :] = halves[1]
        return c

    lax.fori_loop(0, n_blk, split_block, 0)

    def make_stages(diagonal):
        bands = ((0, t // 2, t // 2), (t // 2, t // 2, t)) if diagonal else ((0, t, t),)

        def decode(u):
            step = u >> g_shift
            grp = u & (n_g - 1)
            if diagonal:
                return step, step, grp
            return steps_q_ref[step], steps_k_ref[step], grp

        def band_rows(blk, r0, n):
            start = blk * t + r0
            if not isinstance(start, int):
                start = pl.multiple_of(start, LANES)
            return pl.ds(start, n)

        def logits(u, slot, gi):
            qi, kj, grp = decode(u)
            for r0, nr, nc in bands:
                k = k_ref[grp, band_rows(kj, 0, nc), :]
                for half in range(2):
                    z = _dot_nt(qm_ref[grp, half, band_rows(qi, r0, nr), :], k)
                    if diagonal:
                        z = z + dmask_ref[r0:r0 + nr, :nc]
                    z_ref[slot, 2 * gi + half, r0:r0 + nr, :nc] = z

        def log_terms(u, slot, gi):
            qi, kj, grp = decode(u)
            for r0, nr, nc in bands:
                for half in range(2):
                    head = 2 * grp + half
                    hand = 2 * gi + half
                    z = z_ref[slot, hand, r0:r0 + nr, :nc]
                    sp = jnp.log(1.0 + jnp.exp2(-jnp.abs(z))) * LOG2E
                    neg_log_om = jnp.maximum(z, 0.0) + sp
                    log_beta = z - neg_log_om
                    om_sum = jnp.sum(neg_log_om, axis=1, keepdims=True)
                    x_ref[slot, hand, r0:r0 + nr, :nc] = neg_log_om.astype(_BF16)
                    if diagonal:
                        lb_ref[slot, hand, r0:r0 + nr, :nc] = log_beta
                        c_ref[qi, head, r0:r0 + nr, :] = jnp.broadcast_to(om_sum, (nr, LANES))
                    else:
                        carry = c_ref[qi, head]
                        lb_ref[slot, hand] = log_beta - _lane_tile(carry, reps)
                        c_ref[qi, head] = carry + om_sum

        def weights(u, slot, gi):
            for r0, nr, nc in bands:
                for hand in range(2 * gi, 2 * gi + 2):
                    later = _dot(x_ref[slot, hand, r0:r0 + nr, :nc], tri_ref[:nc, :nc])
                    e = later + lb_ref[slot, hand, r0:r0 + nr, :nc]
                    w_ref[slot, hand, r0:r0 + nr, :nc] = jnp.exp2(e).astype(_BF16)

        def values(u, slot, gi):
            qi, kj, grp = decode(u)
            for r0, nr, nc in bands:
                keys = band_rows(kj, 0, nc)
                pv = (_dot(w_ref[slot, 2 * gi, r0:r0 + nr, :nc], vm_ref[grp, 0, keys, :])
                      + _dot(w_ref[slot, 2 * gi + 1, r0:r0 + nr, :nc], vm_ref[grp, 1, keys, :]))
                if diagonal:
                    acc_ref[qi, grp, r0:r0 + nr, :] = pv
                else:
                    acc_ref[qi, grp] += pv

        def per_unit(stage):
            def run(u, slot):
                for gi in range(UNIT_GROUPS):
                    stage(u * UNIT_GROUPS + gi, slot, gi)
            return run

        return [per_unit(stage) for stage in (logits, log_terms, weights, values)]

    units_per_step = n_g // UNIT_GROUPS
    _skewed_pipeline(n_blk * units_per_step, make_stages(True))
    _skewed_pipeline(steps_q_ref.shape[0] * units_per_step, make_stages(False))

    li = lax.broadcasted_iota(jnp.int32, (LANES, LANES), 0)
    lj = lax.broadcasted_iota(jnp.int32, (LANES, LANES), 1)
    head_mean = jnp.where((li < HEAD_DIM) == (lj < HEAD_DIM), 1.0 / HEAD_DIM, 0.0).astype(_BF16)

    def finish(qi, carry):
        for grp in range(n_g):
            o = acc_ref[qi, grp]
            ms = _dot((o * o).astype(_BF16), head_mean)
            o_ref[grp, _rows(qi), :] = (o * lax.rsqrt(ms + EPS) * g_ref[...]).astype(o_ref.dtype)
        return carry

    lax.fori_loop(0, n_blk, finish, 0)


def _sb_attn(qkv_g, batch, gain2):
    n_tok = qkv_g.shape[1]
    s = n_tok // batch
    n_blk = s // ATTN_TILE
    steps_q, steps_k = _offdiag_steps(n_blk)
    smem = pl.BlockSpec(memory_space=pltpu.SMEM)
    handoff = lambda dtype: pltpu.VMEM((2, 2 * UNIT_GROUPS, ATTN_TILE, ATTN_TILE), dtype)
    return pl.pallas_call(
        _sb_attn_kernel,
        grid=(batch,),
        in_specs=[
            smem, smem,
            pl.BlockSpec((1, LANES), lambda bi: (0, 0)),
            _group_spec(s, 3), _group_spec(s, 4), _group_spec(s, 5),
        ],
        out_specs=pl.BlockSpec((GROUPS, s, LANES), lambda bi: (0, bi, 0)),
        out_shape=jax.ShapeDtypeStruct((GROUPS, n_tok, LANES), _BF16),
        scratch_shapes=[pltpu.VMEM((n_blk, N_SB_HEADS, ATTN_TILE, LANES), _F32),
                        pltpu.VMEM((n_blk, GROUPS, ATTN_TILE, LANES), _F32),
                        handoff(_F32), handoff(_BF16), handoff(_F32), handoff(_BF16),
                        pltpu.VMEM((ATTN_TILE, ATTN_TILE), _BF16),
                        pltpu.VMEM((ATTN_TILE, ATTN_TILE), _F32),
                        pltpu.VMEM((GROUPS, 2, s, LANES), _BF16),
                        pltpu.VMEM((GROUPS, 2, s, LANES), _BF16)],
        compiler_params=pltpu.CompilerParams(vmem_limit_bytes=VMEM_LIMIT_BYTES),
        name="sb_attn",
    )(steps_q, steps_k, gain2, qkv_g, qkv_g, qkv_g)


def _out_mlp_kernel(x_ref, md_ref, ms_ref, wo_ref, g2_ref, wup_ref, wdn_ref,
                    g3_ref, o_ref, m_ref, acc_ref):
    mixed = jnp.concatenate([md_ref[g] for g in range(md_ref.shape[0])]
                            + [ms_ref[g] for g in range(ms_ref.shape[0])], axis=1)
    h = x_ref[...] + _dot(mixed, wo_ref[...])
    m_ref[...] = _rmsnorm(h, g2_ref[...]).astype(_BF16)
    acc_ref[...] = h

    def ff_chunk(c, carry):
        cols = pl.ds(pl.multiple_of(c * FF_CHUNK, FF_CHUNK), FF_CHUNK)
        u = jnp.square(jnp.maximum(_dot(m_ref[...], wup_ref[:, cols]), 0.0))
        acc_ref[...] += _dot(u.astype(_BF16), wdn_ref[cols, :])
        return carry

    lax.fori_loop(0, D_FF // FF_CHUNK, ff_chunk, 0)
    o_ref[...] = _rmsnorm(acc_ref[...], g3_ref[...])


def _out_mlp(x2d, mix_d, mix_s, w_out, g2, w_up, w_dn, g3):
    n_tok = x2d.shape[0]
    const = lambda shape: pl.BlockSpec(shape, lambda i: (0, 0), pipeline_mode=pl.Buffered(1))
    tile = MLP_TOKEN_TILE
    mix = pl.BlockSpec((GROUPS, tile, LANES), lambda i: (0, i, 0))
    return pl.pallas_call(
        _out_mlp_kernel,
        grid=(n_tok // tile,),
        in_specs=[
            pl.BlockSpec((tile, D_MODEL), lambda i: (i, 0)),
            mix, mix,
            const((MIX_WIDTH, D_MODEL)),
            const((1, D_MODEL)),
            const((D_MODEL, D_FF)),
            const((D_FF, D_MODEL)),
            const((1, D_MODEL)),
        ],
        out_specs=pl.BlockSpec((tile, D_MODEL), lambda i: (i, 0)),
        out_shape=jax.ShapeDtypeStruct((n_tok, D_MODEL), _F32),
        scratch_shapes=[pltpu.VMEM((tile, D_MODEL), _BF16),
                        pltpu.VMEM((tile, D_MODEL), _F32)],
        compiler_params=pltpu.CompilerParams(vmem_limit_bytes=VMEM_LIMIT_BYTES),
        name="out_mlp",
    )(x2d, mix_d, mix_s, w_out, g2, w_up, w_dn, g3)


def _query_col_scale():
    qs = ATTN_SCALE * LOG2E
    one = jnp.ones((DIFF_WIDTH,), _F32)
    return jnp.concatenate([qs * one, one, one, qs * one, one, one]).reshape(1, 3 * MIX_WIDTH)


def kernel(x, attn_norm, w_in, lambda_q1, lambda_k1, lambda_q2, lambda_k2, diff_subln, sb_subln,
           w_out, mlp_norm, w_up, w_down, final_norm):
    assert attn_norm.shape[0] == 1, "single-layer block"
    b, s, d = x.shape
    x2d = x.reshape(b * s, d)

    qkv_g = _in_proj(x2d, attn_norm, w_in[0].astype(_BF16), _query_col_scale())

    slopes2 = jnp.asarray([LOG2E * 2.0 ** (-8.0 * (h + 1) / N_DIFF_HEADS)
                           for h in range(N_DIFF_HEADS)], dtype=_F32)
    mix_d = _diff_attn(qkv_g, b, slopes2, lambda_q1, lambda_k1, lambda_q2, lambda_k2, diff_subln)
    mix_s = _sb_attn(qkv_g, b, jnp.tile(sb_subln, (1, 2)))

    out = _out_mlp(x2d, mix_d, mix_s, w_out[0].astype(_BF16), mlp_norm,
                   w_up[0].astype(_BF16), w_down[0].astype(_BF16), final_norm.reshape(1, d))
    return out.reshape(b, s, d)
```

```python
import math

import jax
import jax.numpy as jnp
import numpy as np
from jax import lax
from jax.experimental import pallas as pl
from jax.experimental.pallas import tpu as pltpu

D_MODEL = 1024
HEAD_DIM = 64
DIFF_WIDTH = D_MODEL // 2
N_DIFF_HEADS = DIFF_WIDTH // (2 * HEAD_DIM)
SB_WIDTH = D_MODEL - DIFF_WIDTH
N_SB_HEADS = SB_WIDTH // HEAD_DIM
N_SB_PAIRS = N_SB_HEADS // 2
MIX_WIDTH = DIFF_WIDTH + SB_WIDTH
D_FF = 4 * D_MODEL
EPS = 1e-6
LAYER_IDX = 0
LAM_INIT = 0.8 - 0.6 * math.exp(-0.3 * LAYER_IDX)
ATTN_SCALE = 1.0 / math.sqrt(HEAD_DIM)
LOG2E = math.log2(math.e)

LANES = 128
GROUPS = 4
ATTN_TILE = 256
UNIT_GROUPS = 1
PIPELINE_UNROLL = 16
TOKEN_TILE = 1024
MLP_TOKEN_TILE = 1024
FF_CHUNK = 1024
MASKED = -1e30
VMEM_LIMIT_BYTES = 56 * 1024 * 1024

_F32 = jnp.float32
_BF16 = jnp.bfloat16


def _rmsnorm(x, gain):
    inv = lax.rsqrt(jnp.mean(x * x, axis=-1, keepdims=True) + EPS)
    return x * inv * gain


def _dot(a, b):
    return jnp.dot(a, b, preferred_element_type=_F32)


def _dot_nt(a, b):
    return lax.dot_general(a, b, (((1,), (1,)), ((), ())), preferred_element_type=_F32)


def _lane_tile(x, reps):
    return jnp.concatenate([x] * reps, axis=1)


def _in_proj_kernel(x_ref, g_ref, w_ref, cs_ref, o_ref):
    a = _rmsnorm(x_ref[...], g_ref[...])
    res = (_dot(a.astype(_BF16), w_ref[...]) * cs_ref[...]).astype(_BF16)
    for g in range(o_ref.shape[0]):
        o_ref[g] = res[:, g * LANES:(g + 1) * LANES]


def _in_proj(x2d, gain, w_bf16, col_scale):
    n_tok = x2d.shape[0]
    n_out = w_bf16.shape[1]
    n_groups = n_out // LANES
    return pl.pallas_call(
        _in_proj_kernel,
        grid=(n_tok // TOKEN_TILE,),
        in_specs=[
            pl.BlockSpec((TOKEN_TILE, D_MODEL), lambda i: (i, 0)),
            pl.BlockSpec((1, D_MODEL), lambda i: (0, 0)),
            pl.BlockSpec((D_MODEL, n_out), lambda i: (0, 0)),
            pl.BlockSpec((1, n_out), lambda i: (0, 0)),
        ],
        out_specs=pl.BlockSpec((n_groups, TOKEN_TILE, LANES), lambda i: (0, i, 0)),
        out_shape=jax.ShapeDtypeStruct((n_groups, n_tok, LANES), _BF16),
        compiler_params=pltpu.CompilerParams(vmem_limit_bytes=VMEM_LIMIT_BYTES),
        name="in_proj",
    )(x2d, gain, w_bf16, col_scale)


def _split_halves(x_bf16):
    lane = lax.broadcasted_iota(jnp.int32, (1, LANES), 1)
    zero = jnp.zeros_like(x_bf16)
    return (jnp.where(lane < HEAD_DIM, x_bf16, zero), jnp.where(lane >= HEAD_DIM, x_bf16, zero))


def _rows(blk):
    start = blk * ATTN_TILE
    if not isinstance(start, int):
        start = pl.multiple_of(start, ATTN_TILE)
    return pl.ds(start, ATTN_TILE)


def _as_f32(i):
    return jnp.asarray(i, jnp.int32).astype(_F32)


def _group_spec(seq, section):
    return pl.BlockSpec((GROUPS, seq, LANES), lambda bi: (section, bi, 0))


def _offdiag_steps(n_blk):
    steps = [(qi, kj) for qi in range(1, n_blk) for kj in range(qi - 1, -1, -1)]
    return (jnp.asarray(np.array([qk[0] for qk in steps], np.int32)),
            jnp.asarray(np.array([qk[1] for qk in steps], np.int32)))


def _skewed_pipeline(n_units, stages):
    depth = len(stages)
    assert n_units >= depth

    def iteration(j, parity, lo, hi):
        for k in range(depth - 1, -1, -1):
            if lo <= k < hi:
                stages[k](j - k, (parity - k) % 2)

    for j in range(depth - 1):
        iteration(j, j % 2, 0, j + 1)

    start = depth - 1
    n_trips, left = divmod(n_units - start, PIPELINE_UNROLL)

    def body(i, c):
        for k in range(PIPELINE_UNROLL):
            iteration(start + PIPELINE_UNROLL * i + k, (start + k) % 2, 0, depth)
        return c

    lax.fori_loop(0, n_trips, body, 0)
    for j in range(n_units - left, n_units):
        iteration(j, j % 2, 0, depth)
    for j in range(n_units, n_units + depth - 1):
        iteration(j, j % 2, j - n_units + 1, depth)


def _diff_attn_kernel(slopes_ref, lq1_ref, lk1_ref, lq2_ref, lk2_ref,
                      g_ref, q_ref, k_ref, v_ref, o_ref, m_ref, a_ref, dbias_ref):
    t = ATTN_TILE
    n_blk = q_ref.shape[1] // t
    n_h = q_ref.shape[0]

    lam = (jnp.exp(jnp.sum(lq1_ref[...] * lk1_ref[...], axis=-1, keepdims=True))
           - jnp.exp(jnp.sum(lq2_ref[...] * lk2_ref[...], axis=-1, keepdims=True))
           + LAM_INIT)

    assert n_blk % 2 == 0
    wide = 2 * t
    row = lax.broadcasted_iota(jnp.int32, (t, t), 0)
    col = lax.broadcasted_iota(jnp.int32, (t, t), 1)
    colf = lax.broadcasted_iota(jnp.int32, (1, wide), 1).astype(_F32)
    for h in range(n_h):
        diag = jnp.where(row >= col, slopes_ref[h] * col.astype(_F32), MASKED)
        dbias_ref[h, :, t:] = diag
        dbias_ref[h, :, :t] = jnp.broadcast_to(slopes_ref[h] * (colf[:, :t] - t), (t, t))

    def key_step(kind, qi, k0, base):
        first = kind != "full"
        width = t if kind == "diag" else wide
        ones = jnp.ones((width, LANES), _BF16)
        if not isinstance(k0, int):
            k0 = pl.multiple_of(k0, t)
        for head in range(n_h):
            q_halves = _split_halves(q_ref[head, _rows(qi), :])
            k = k_ref[head, pl.ds(k0, width), :]
            v_aug = jnp.concatenate([v_ref[head, pl.ds(k0, width), :], ones], axis=1)
            if kind == "diag":
                bias = dbias_ref[head, :, t:]
            elif kind == "pair_diag":
                bias = dbias_ref[head]
            else:
                bias = slopes_ref[head] * (colf + _as_f32(k0 - qi * t))
            for half in range(2):
                idx = base + 2 * head + half
                s = _dot_nt(q_halves[half], k) + bias
                m_curr = jnp.max(s, axis=1, keepdims=True)
                if first:
                    m_new = jnp.broadcast_to(m_curr, (t, LANES))
                else:
                    m_prev = m_ref[idx]
                    m_new = jnp.maximum(m_prev, m_curr)
                p = jnp.exp2(s - _lane_tile(m_new, width // LANES))
                pv = _dot(p.astype(_BF16), v_aug)
                if first:
                    a_ref[idx] = pv
                else:
                    a_ref[idx] = _lane_tile(jnp.exp2(m_prev - m_new), 2) * a_ref[idx] + pv
                m_ref[idx] = m_new

    def finish(qi, base):
        for h in range(n_h):
            a1 = a_ref[base + 2 * h]
            a2 = a_ref[base + 2 * h + 1]
            o = a1[:, :LANES] / a1[:, LANES:] - lam * (a2[:, :LANES] / a2[:, LANES:])
            o = _rmsnorm(o, g_ref[...]) * (1.0 - LAM_INIT)
            o_ref[h, _rows(qi), :] = o.astype(o_ref.dtype)

    for r in range(n_blk // 2):
        base = (r % 2) * 4 * n_h
        key_step("diag", 2 * r, 2 * r * t, base)
        key_step("pair_diag", 2 * r + 1, 2 * r * t, base + 2 * n_h)
        if r > 0:
            prev = ((r - 1) % 2) * 4 * n_h
            finish(2 * r - 2, prev)
            finish(2 * r - 1, prev + 2 * n_h)

        for c in range(r):
            key_step("full", 2 * r, c * wide, base)
            key_step("full", 2 * r + 1, c * wide, base + 2 * n_h)
    last = ((n_blk // 2 - 1) % 2) * 4 * n_h
    finish(n_blk - 2, last)
    finish(n_blk - 1, last + 2 * n_h)


def _diff_attn(qkv_g, batch, slopes2, lq1, lk1, lq2, lk2, gain):
    n_tok = qkv_g.shape[1]
    s = n_tok // batch
    n_h = N_DIFF_HEADS
    vec = lambda n: pl.BlockSpec((1, n), lambda bi: (0, 0))
    return pl.pallas_call(
        _diff_attn_kernel,
        grid=(batch,),
        in_specs=[
            pl.BlockSpec(memory_space=pltpu.SMEM),
            vec(HEAD_DIM), vec(HEAD_DIM), vec(HEAD_DIM), vec(HEAD_DIM), vec(LANES),
            _group_spec(s, 0), _group_spec(s, 1), _group_spec(s, 2),
        ],
        out_specs=pl.BlockSpec((GROUPS, s, LANES), lambda bi: (0, bi, 0)),
        out_shape=jax.ShapeDtypeStruct((GROUPS, n_tok, LANES), _BF16),
        scratch_shapes=[pltpu.VMEM((8 * n_h, ATTN_TILE, LANES), _F32),
                        pltpu.VMEM((8 * n_h, ATTN_TILE, 2 * LANES), _F32),
                        pltpu.VMEM((n_h, ATTN_TILE, 2 * ATTN_TILE), _F32)],
        compiler_params=pltpu.CompilerParams(vmem_limit_bytes=VMEM_LIMIT_BYTES),
        name="diff_attn",
    )(slopes2, lq1, lk1, lq2, lk2, gain, qkv_g, qkv_g, qkv_g)


def _sb_attn_kernel(steps_q_ref, steps_k_ref, g_ref, q_ref, k_ref, v_ref, o_ref,
                    c_ref, acc_ref, z_ref, x_ref, lb_ref, w_ref, tri_ref, dmask_ref,
                    qm_ref, vm_ref):
    t = ATTN_TILE
    reps = t // LANES
    n_blk = q_ref.shape[1] // t
    n_g = q_ref.shape[0]
    g_shift = n_g.bit_length() - 1
    assert n_g == 1 << g_shift

    row = lax.broadcasted_iota(jnp.int32, (t, t), 0)
    col = lax.broadcasted_iota(jnp.int32, (t, t), 1)
    tri_ref[...] = jnp.where(row > col, -1.0, 0.0).astype(_BF16)
    dmask_ref[...] = jnp.where(col < row, 0.0, MASKED)

    def split_block(j, c):
        for grp in range(n_g):
            for src_ref, dst_ref in ((q_ref, qm_ref), (v_ref, vm_ref)):
                halves = _split_halves(src_ref[grp, _rows(j), :])
                dst_ref[grp, 0, _rows(j), :] = halves[0]
                dst_ref[grp, 1, _rows(j), :] = halves[1]
        return c

    lax.fori_loop(0, n_blk, split_block, 0)

    def make_stages(diagonal):
        bands = ((0, t // 2, t // 2), (t // 2, t // 2, t)) if diagonal else ((0, t, t),)

        def decode(u):
            step = u >> g_shift
            grp = u & (n_g - 1)
            if diagonal:
                return step, step, grp
            return steps_q_ref[step], steps_k_ref[step], grp

        def band_rows(blk, r0, n):
            start = blk * t + r0
            if not isinstance(start, int):
                start = pl.multiple_of(start, LANES)
            return pl.ds(start, n)

        def logits(u, slot, gi):
            qi, kj, grp = decode(u)
            for r0, nr, nc in bands:
                k = k_ref[grp, band_rows(kj, 0, nc), :]
                for half in range(2):
                    z = _dot_nt(qm_ref[grp, half, band_rows(qi, r0, nr), :], k)
                    if diagonal:
                        z = z + dmask_ref[r0:r0 + nr, :nc]
                    z_ref[slot, 2 * gi + half, r0:r0 + nr, :nc] = z

        def log_terms(u, slot, gi):
            qi, kj, grp = decode(u)
            for r0, nr, nc in bands:
                for half in range(2):
                    head = 2 * grp + half
                    hand = 2 * gi + half
                    z = z_ref[slot, hand, r0:r0 + nr, :nc]
                    sp = jnp.log(1.0 + jnp.exp2(-jnp.abs(z))) * LOG2E
                    neg_log_om = jnp.maximum(z, 0.0) + sp
                    log_beta = z - neg_log_om
                    om_sum = jnp.sum(neg_log_om, axis=1, keepdims=True)
                    x_ref[slot, hand, r0:r0 + nr, :nc] = neg_log_om.astype(_BF16)
                    if diagonal:
                        lb_ref[slot, hand, r0:r0 + nr, :nc] = log_beta
                        c_ref[qi, head, r0:r0 + nr, :] = jnp.broadcast_to(om_sum, (nr, LANES))
                    else:
                        carry = c_ref[qi, head]
                        lb_ref[slot, hand] = log_beta - _lane_tile(carry, reps)
                        c_ref[qi, head] = carry + om_sum

        def weights(u, slot, gi):
            for r0, nr, nc in bands:
                for hand in range(2 * gi, 2 * gi + 2):
                    later = _dot(x_ref[slot, hand, r0:r0 + nr, :nc], tri_ref[:nc, :nc])
                    e = later + lb_ref[slot, hand, r0:r0 + nr, :nc]
                    w_ref[slot, hand, r0:r0 + nr, :nc] = jnp.exp2(e).astype(_BF16)

        def values(u, slot, gi):
            qi, kj, grp = decode(u)
            for r0, nr, nc in bands:
                keys = band_rows(kj, 0, nc)
                pv = (_dot(w_ref[slot, 2 * gi, r0:r0 + nr, :nc], vm_ref[grp, 0, keys, :])
                      + _dot(w_ref[slot, 2 * gi + 1, r0:r0 + nr, :nc], vm_ref[grp, 1, keys, :]))
                if diagonal:
                    acc_ref[qi, grp, r0:r0 + nr, :] = pv
                else:
                    acc_ref[qi, grp] += pv

        def per_unit(stage):
            def run(u, slot):
                for gi in range(UNIT_GROUPS):
                    stage(u * UNIT_GROUPS + gi, slot, gi)
            return run

        return [per_unit(stage) for stage in (logits, log_terms, weights, values)]

    units_per_step = n_g // UNIT_GROUPS
    _skewed_pipeline(n_blk * units_per_step, make_stages(True))
    _skewed_pipeline(steps_q_ref.shape[0] * units_per_step, make_stages(False))

    li = lax.broadcasted_iota(jnp.int32, (LANES, LANES), 0)
    lj = lax.broadcasted_iota(jnp.int32, (LANES, LANES), 1)
    head_mean = jnp.where((li < HEAD_DIM) == (lj < HEAD_DIM), 1.0 / HEAD_DIM, 0.0).astype(_BF16)

    def finish(qi, carry):
        for grp in range(n_g):
            o = acc_ref[qi, grp]
            ms = _dot((o * o).astype(_BF16), head_mean)
            o_ref[grp, _rows(qi), :] = (o * lax.rsqrt(ms + EPS) * g_ref[...]).astype(o_ref.dtype)
        return carry

    lax.fori_loop(0, n_blk, finish, 0)


def _sb_attn(qkv_g, batch, gain2):
    n_tok = qkv_g.shape[1]
    s = n_tok // batch
    n_blk = s // ATTN_TILE
    steps_q, steps_k = _offdiag_steps(n_blk)
    smem = pl.BlockSpec(memory_space=pltpu.SMEM)
    handoff = lambda dtype: pltpu.VMEM((2, 2 * UNIT_GROUPS, ATTN_TILE, ATTN_TILE), dtype)
    return pl.pallas_call(
        _sb_attn_kernel,
        grid=(batch,),
        in_specs=[
            smem, smem,
            pl.BlockSpec((1, LANES), lambda bi: (0, 0)),
            _group_spec(s, 3), _group_spec(s, 4), _group_spec(s, 5),
        ],
        out_specs=pl.BlockSpec((GROUPS, s, LANES), lambda bi: (0, bi, 0)),
        out_shape=jax.ShapeDtypeStruct((GROUPS, n_tok, LANES), _BF16),
        scratch_shapes=[pltpu.VMEM((n_blk, N_SB_HEADS, ATTN_TILE, LANES), _F32),
                        pltpu.VMEM((n_blk, GROUPS, ATTN_TILE, LANES), _F32),
                        handoff(_F32), handoff(_BF16), handoff(_F32), handoff(_BF16),
                        pltpu.VMEM((ATTN_TILE, ATTN_TILE), _BF16),
                        pltpu.VMEM((ATTN_TILE, ATTN_TILE), _F32),
                        pltpu.VMEM((GROUPS, 2, s, LANES), _BF16),
                        pltpu.VMEM((GROUPS, 2, s, LANES), _BF16)],
        compiler_params=pltpu.CompilerParams(vmem_limit_bytes=VMEM_LIMIT_BYTES),
        name="sb_attn",
    )(steps_q, steps_k, gain2, qkv_g, qkv_g, qkv_g)


def _out_mlp_kernel(x_ref, md_ref, ms_ref, wo_ref, g2_ref, wup_ref, wdn_ref,
                    g3_ref, o_ref, m_ref, acc_ref):
    mixed = jnp.concatenate([md_ref[g] for g in range(md_ref.shape[0])]
                            + [ms_ref[g] for g in range(ms_ref.shape[0])], axis=1)
    h = x_ref[...] + _dot(mixed, wo_ref[...])
    m_ref[...] = _rmsnorm(h, g2_ref[...]).astype(_BF16)
    acc_ref[...] = h

    def ff_chunk(c, carry):
        cols = pl.ds(pl.multiple_of(c * FF_CHUNK, FF_CHUNK), FF_CHUNK)
        u = jnp.square(jnp.maximum(_dot(m_ref[...], wup_ref[:, cols]), 0.0))
        acc_ref[...] += _dot(u.astype(_BF16), wdn_ref[cols, :])
        return carry

    lax.fori_loop(0, D_FF // FF_CHUNK, ff_chunk, 0)
    o_ref[...] = _rmsnorm(acc_ref[...], g3_ref[...])


def _out_mlp(x2d, mix_d, mix_s, w_out, g2, w_up, w_dn, g3):
    n_tok = x2d.shape[0]
    const = lambda shape: pl.BlockSpec(shape, lambda i: (0, 0), pipeline_mode=pl.Buffered(1))
    tile = MLP_TOKEN_TILE
    mix = pl.BlockSpec((GROUPS, tile, LANES), lambda i: (0, i, 0))
    return pl.pallas_call(
        _out_mlp_kernel,
        grid=(n_tok // tile,),
        in_specs=[
            pl.BlockSpec((tile, D_MODEL), lambda i: (i, 0)),
            mix, mix,
            const((MIX_WIDTH, D_MODEL)),
            const((1, D_MODEL)),
            const((D_MODEL, D_FF)),
            const((D_FF, D_MODEL)),
            const((1, D_MODEL)),
        ],
        out_specs=pl.BlockSpec((tile, D_MODEL), lambda i: (i, 0)),
        out_shape=jax.ShapeDtypeStruct((n_tok, D_MODEL), _F32),
        scratch_shapes=[pltpu.VMEM((tile, D_MODEL), _BF16),
                        pltpu.VMEM((tile, D_MODEL), _F32)],
        compiler_params=pltpu.CompilerParams(vmem_limit_bytes=VMEM_LIMIT_BYTES),
        name="out_mlp",
    )(x2d, mix_d, mix_s, w_out, g2, w_up, w_dn, g3)


def _query_col_scale():
    qs = ATTN_SCALE * LOG2E
    one = jnp.ones((DIFF_WIDTH,), _F32)
    return jnp.concatenate([qs * one, one, one, qs * one, one, one]).reshape(1, 3 * MIX_WIDTH)


def kernel(x, attn_norm, w_in, lambda_q1, lambda_k1, lambda_q2, lambda_k2, diff_subln, sb_subln,
           w_out, mlp_norm, w_up, w_down, final_norm):
    assert attn_norm.shape[0] == 1, "single-layer block"
    b, s, d = x.shape
    x2d = x.reshape(b * s, d)

    qkv_g = _in_proj(x2d, attn_norm, w_in[0].astype(_BF16), _query_col_scale())

    slopes2 = jnp.asarray([LOG2E * 2.0 ** (-8.0 * (h + 1) / N_DIFF_HEADS)
                           for h in range(N_DIFF_HEADS)], dtype=_F32)
    mix_d = _diff_attn(qkv_g, b, slopes2, lambda_q1, lambda_k1, lambda_q2, lambda_k2, diff_subln)
    mix_s = _sb_attn(qkv_g, b, jnp.tile(sb_subln, (1, 2)))

    out = _out_mlp(x2d, mix_d, mix_s, w_out[0].astype(_BF16), mlp_norm,
                   w_up[0].astype(_BF16), w_down[0].astype(_BF16), final_norm.reshape(1, d))
    return out.reshape(b, s, d)
```

```python
import math

import jax
import jax.numpy as jnp
import numpy as np
from jax import lax
from jax.experimental import pallas as pl
from jax.experimental.pallas import tpu as pltpu

D_MODEL = 1024
HEAD_DIM = 64
DIFF_WIDTH = D_MODEL // 2
N_DIFF_HEADS = DIFF_WIDTH // (2 * HEAD_DIM)
SB_WIDTH = D_MODEL - DIFF_WIDTH
N_SB_HEADS = SB_WIDTH // HEAD_DIM
N_SB_PAIRS = N_SB_HEADS // 2
MIX_WIDTH = DIFF_WIDTH + SB_WIDTH
D_FF = 4 * D_MODEL
EPS = 1e-6
LAYER_IDX = 0
LAM_INIT = 0.8 - 0.6 * math.exp(-0.3 * LAYER_IDX)
ATTN_SCALE = 1.0 / math.sqrt(HEAD_DIM)
LOG2E = math.log2(math.e)

LANES = 128
GROUPS = 4
ATTN_TILE = 256
UNIT_GROUPS = 1
PIPELINE_UNROLL = 24
TOKEN_TILE = 1024
MLP_TOKEN_TILE = 1024
FF_CHUNK = 1024
MASKED = -1e30
VMEM_LIMIT_BYTES = 56 * 1024 * 1024

_F32 = jnp.float32
_BF16 = jnp.bfloat16


def _rmsnorm(x, gain):
    inv = lax.rsqrt(jnp.mean(x * x, axis=-1, keepdims=True) + EPS)
    return x * inv * gain


def _dot(a, b):
    return jnp.dot(a, b, preferred_element_type=_F32)


def _dot_nt(a, b):
    return lax.dot_general(a, b, (((1,), (1,)), ((), ())), preferred_element_type=_F32)


def _lane_tile(x, reps):
    return jnp.concatenate([x] * reps, axis=1)


def _in_proj_kernel(x_ref, g_ref, w_ref, cs_ref, o_ref):
    a = _rmsnorm(x_ref[...], g_ref[...])
    res = (_dot(a.astype(_BF16), w_ref[...]) * cs_ref[...]).astype(_BF16)
    for g in range(o_ref.shape[0]):
        o_ref[g] = res[:, g * LANES:(g + 1) * LANES]


def _in_proj(x2d, gain, w_bf16, col_scale):
    n_tok = x2d.shape[0]
    n_out = w_bf16.shape[1]
    n_groups = n_out // LANES
    return pl.pallas_call(
        _in_proj_kernel,
        grid=(n_tok // TOKEN_TILE,),
        in_specs=[
            pl.BlockSpec((TOKEN_TILE, D_MODEL), lambda i: (i, 0)),
            pl.BlockSpec((1, D_MODEL), lambda i: (0, 0)),
            pl.BlockSpec((D_MODEL, n_out), lambda i: (0, 0)),
            pl.BlockSpec((1, n_out), lambda i: (0, 0)),
        ],
        out_specs=pl.BlockSpec((n_groups, TOKEN_TILE, LANES), lambda i: (0, i, 0)),
        out_shape=jax.ShapeDtypeStruct((n_groups, n_tok, LANES), _BF16),
        compiler_params=pltpu.CompilerParams(vmem_limit_bytes=VMEM_LIMIT_BYTES),
        name="in_proj",
    )(x2d, gain, w_bf16, col_scale)


def _split_halves(x_bf16):
    lane = lax.broadcasted_iota(jnp.int32, (1, LANES), 1)
    zero = jnp.zeros_like(x_bf16)
    return (jnp.where(lane < HEAD_DIM, x_bf16, zero), jnp.where(lane >= HEAD_DIM, x_bf16, zero))


def _rows(blk):
    start = blk * ATTN_TILE
    if not isinstance(start, int):
        start = pl.multiple_of(start, ATTN_TILE)
    return pl.ds(start, ATTN_TILE)


def _as_f32(i):
    return jnp.asarray(i, jnp.int32).astype(_F32)


def _group_spec(seq, section):
    return pl.BlockSpec((GROUPS, seq, LANES), lambda bi: (section, bi, 0))


def _offdiag_steps(n_blk):
    steps = [(qi, kj) for qi in range(1, n_blk) for kj in range(qi - 1, -1, -1)]
    return (jnp.asarray(np.array([qk[0] for qk in steps], np.int32)),
            jnp.asarray(np.array([qk[1] for qk in steps], np.int32)))


def _skewed_pipeline(n_units, stages):
    depth = len(stages)
    assert n_units >= depth

    def iteration(j, parity, lo, hi):
        for k in range(depth - 1, -1, -1):
            if lo <= k < hi:
                stages[k](j - k, (parity - k) % 2)

    for j in range(depth - 1):
        iteration(j, j % 2, 0, j + 1)

    start = depth - 1
    n_trips, left = divmod(n_units - start, PIPELINE_UNROLL)

    def body(i, c):
        for k in range(PIPELINE_UNROLL):
            iteration(start + PIPELINE_UNROLL * i + k, (start + k) % 2, 0, depth)
        return c

    lax.fori_loop(0, n_trips, body, 0)
    for j in range(n_units - left, n_units):
        iteration(j, j % 2, 0, depth)
    for j in range(n_units, n_units + depth - 1):
        iteration(j, j % 2, j - n_units + 1, depth)


def _diff_attn_kernel(slopes_ref, lq1_ref, lk1_ref, lq2_ref, lk2_ref,
                      g_ref, q_ref, k_ref, v_ref, o_ref, m_ref, a_ref, dbias_ref):
    t = ATTN_TILE
    n_blk = q_ref.shape[1] // t
    n_h = q_ref.shape[0]

    lam = (jnp.exp(jnp.sum(lq1_ref[...] * lk1_ref[...], axis=-1, keepdims=True))
           - jnp.exp(jnp.sum(lq2_ref[...] * lk2_ref[...], axis=-1, keepdims=True))
           + LAM_INIT)

    assert n_blk % 2 == 0
    wide = 2 * t
    row = lax.broadcasted_iota(jnp.int32, (t, t), 0)
    col = lax.broadcasted_iota(jnp.int32, (t, t), 1)
    colf = lax.broadcasted_iota(jnp.int32, (1, wide), 1).astype(_F32)
    for h in range(n_h):
        diag = jnp.where(row >= col, slopes_ref[h] * col.astype(_F32), MASKED)
        dbias_ref[h, :, t:] = diag
        dbias_ref[h, :, :t] = jnp.broadcast_to(slopes_ref[h] * (colf[:, :t] - t), (t, t))

    def key_step(kind, qi, k0, base):
        first = kind != "full"
        width = t if kind == "diag" else wide
        ones = jnp.ones((width, LANES), _BF16)
        if not isinstance(k0, int):
            k0 = pl.multiple_of(k0, t)
        for head in range(n_h):
            q_halves = _split_halves(q_ref[head, _rows(qi), :])
            k = k_ref[head, pl.ds(k0, width), :]
            v_aug = jnp.concatenate([v_ref[head, pl.ds(k0, width), :], ones], axis=1)
            if kind == "diag":
                bias = dbias_ref[head, :, t:]
            elif kind == "pair_diag":
                bias = dbias_ref[head]
            else:
                bias = slopes_ref[head] * (colf + _as_f32(k0 - qi * t))
            for half in range(2):
                idx = base + 2 * head + half
                s = _dot_nt(q_halves[half], k) + bias
                m_curr = jnp.max(s, axis=1, keepdims=True)
                if first:
                    m_new = jnp.broadcast_to(m_curr, (t, LANES))
                else:
                    m_prev = m_ref[idx]
                    m_new = jnp.maximum(m_prev, m_curr)
                p = jnp.exp2(s - _lane_tile(m_new, width // LANES))
                pv = _dot(p.astype(_BF16), v_aug)
                if first:
                    a_ref[idx] = pv
                else:
                    a_ref[idx] = _lane_tile(jnp.exp2(m_prev - m_new), 2) * a_ref[idx] + pv
                m_ref[idx] = m_new

    def finish(qi, base):
        for h in range(n_h):
            a1 = a_ref[base + 2 * h]
            a2 = a_ref[base + 2 * h + 1]
            o = a1[:, :LANES] / a1[:, LANES:] - lam * (a2[:, :LANES] / a2[:, LANES:])
            o = _rmsnorm(o, g_ref[...]) * (1.0 - LAM_INIT)
            o_ref[h, _rows(qi), :] = o.astype(o_ref.dtype)

    for r in range(n_blk // 2):
        base = (r % 2) * 4 * n_h
        key_step("diag", 2 * r, 2 * r * t, base)
        key_step("pair_diag", 2 * r + 1, 2 * r * t, base + 2 * n_h)
        if r > 0:
            prev = ((r - 1) % 2) * 4 * n_h
            finish(2 * r - 2, prev)
            finish(2 * r - 1, prev + 2 * n_h)

        for c in range(r):
            key_step("full", 2 * r, c * wide, base)
            key_step("full", 2 * r + 1, c * wide, base + 2 * n_h)
    last = ((n_blk // 2 - 1) % 2) * 4 * n_h
    finish(n_blk - 2, last)
    finish(n_blk - 1, last + 2 * n_h)


def _diff_attn(qkv_g, batch, slopes2, lq1, lk1, lq2, lk2, gain):
    n_tok = qkv_g.shape[1]
    s = n_tok // batch
    n_h = N_DIFF_HEADS
    vec = lambda n: pl.BlockSpec((1, n), lambda bi: (0, 0))
    return pl.pallas_call(
        _diff_attn_kernel,
        grid=(batch,),
        in_specs=[
            pl.BlockSpec(memory_space=pltpu.SMEM),
            vec(HEAD_DIM), vec(HEAD_DIM), vec(HEAD_DIM), vec(HEAD_DIM), vec(LANES),
            _group_spec(s, 0), _group_spec(s, 1), _group_spec(s, 2),
        ],
        out_specs=pl.BlockSpec((GROUPS, s, LANES), lambda bi: (0, bi, 0)),
        out_shape=jax.ShapeDtypeStruct((GROUPS, n_tok, LANES), _BF16),
        scratch_shapes=[pltpu.VMEM((8 * n_h, ATTN_TILE, LANES), _F32),
                        pltpu.VMEM((8 * n_h, ATTN_TILE, 2 * LANES), _F32),
                        pltpu.VMEM((n_h, ATTN_TILE, 2 * ATTN_TILE), _F32)],
        compiler_params=pltpu.CompilerParams(vmem_limit_bytes=VMEM_LIMIT_BYTES),
        name="diff_attn",
    )(slopes2, lq1, lk1, lq2, lk2, gain, qkv_g, qkv_g, qkv_g)


def _sb_attn_kernel(steps_q_ref, steps_k_ref, g_ref, q_ref, k_ref, v_ref, o_ref,
                    c_ref, acc_ref, z_ref, x_ref, lb_ref, w_ref, tri_ref, dmask_ref,
                    qm_ref, vm_ref):
    t = ATTN_TILE
    reps = t // LANES
    n_blk = q_ref.shape[1] // t
    n_g = q_ref.shape[0]
    g_shift = n_g.bit_length() - 1
    assert n_g == 1 << g_shift

    row = lax.broadcasted_iota(jnp.int32, (t, t), 0)
    col = lax.broadcasted_iota(jnp.int32, (t, t), 1)
    tri_ref[...] = jnp.where(row > col, -1.0, 0.0).astype(_BF16)
    dmask_ref[...] = jnp.where(col < row, 0.0, MASKED)

    def split_block(j, c):
        for grp in range(n_g):
            for src_ref, dst_ref in ((q_ref, qm_ref), (v_ref, vm_ref)):
                halves = _split_halves(src_ref[grp, _rows(j), :])
                dst_ref[grp, 0, _rows(j), :] = halves[0]
                dst_ref[grp, 1, _rows(j), :] = halves[1]
        return c

    lax.fori_loop(0, n_blk, split_block, 0)

    def make_stages(diagonal):
        bands = ((0, t // 2, t // 2), (t // 2, t // 2, t)) if diagonal else ((0, t, t),)

        def decode(u):
            step = u >> g_shift
            grp = u & (n_g - 1)
            if diagonal:
                return step, step, grp
            return steps_q_ref[step], steps_k_ref[step], grp

        def band_rows(blk, r0, n):
            start = blk * t + r0
            if not isinstance(start, int):
                start = pl.multiple_of(start, LANES)
            return pl.ds(start, n)

        def logits(u, slot, gi):
            qi, kj, grp = decode(u)
            for r0, nr, nc in bands:
                k = k_ref[grp, band_rows(kj, 0, nc), :]
                for half in range(2):
                    z = _dot_nt(qm_ref[grp, half, band_rows(qi, r0, nr), :], k)
                    if diagonal:
                        z = z + dmask_ref[r0:r0 + nr, :nc]
                    z_ref[slot, 2 * gi + half, r0:r0 + nr, :nc] = z

        def log_terms(u, slot, gi):
            qi, kj, grp = decode(u)
            for r0, nr, nc in bands:
                for half in range(2):
                    head = 2 * grp + half
                    hand = 2 * gi + half
                    z = z_ref[slot, hand, r0:r0 + nr, :nc]
                    sp = jnp.log(1.0 + jnp.exp2(-jnp.abs(z))) * LOG2E
                    neg_log_om = jnp.maximum(z, 0.0) + sp
                    log_beta = z - neg_log_om
                    om_sum = jnp.sum(neg_log_om, axis=1, keepdims=True)
                    x_ref[slot, hand, r0:r0 + nr, :nc] = neg_log_om.astype(_BF16)
                    if diagonal:
                        lb_ref[slot, hand, r0:r0 + nr, :nc] = log_beta
                        c_ref[qi, head, r0:r0 + nr, :] = jnp.broadcast_to(om_sum, (nr, LANES))
                    else:
                        carry = c_ref[qi, head]
                        lb_ref[slot, hand] = log_beta - _lane_tile(carry, reps)
                        c_ref[qi, head] = carry + om_sum

        def weights(u, slot, gi):
            for r0, nr, nc in bands:
                for hand in range(2 * gi, 2 * gi + 2):
                    later = _dot(x_ref[slot, hand, r0:r0 + nr, :nc], tri_ref[:nc, :nc])
                    e = later + lb_ref[slot, hand, r0:r0 + nr, :nc]
                    w_ref[slot, hand, r0:r0 + nr, :nc] = jnp.exp2(e).astype(_BF16)

        def values(u, slot, gi):
            qi, kj, grp = decode(u)
            for r0, nr, nc in bands:
                keys = band_rows(kj, 0, nc)
                pv = (_dot(w_ref[slot, 2 * gi, r0:r0 + nr, :nc], vm_ref[grp, 0, keys, :])
                      + _dot(w_ref[slot, 2 * gi + 1, r0:r0 + nr, :nc], vm_ref[grp, 1, keys, :]))
                if diagonal:
                    acc_ref[qi, grp, r0:r0 + nr, :] = pv
                else:
                    acc_ref[qi, grp] += pv

        def per_unit(stage):
            def run(u, slot):
                for gi in range(UNIT_GROUPS):
                    stage(u * UNIT_GROUPS + gi, slot, gi)
            return run

        return [per_unit(stage) for stage in (logits, log_terms, weights, values)]

    units_per_step = n_g // UNIT_GROUPS
    _skewed_pipeline(n_blk * units_per_step, make_stages(True))
    _skewed_pipeline(steps_q_ref.shape[0] * units_per_step, make_stages(False))

    li = lax.broadcasted_iota(jnp.int32, (LANES, LANES), 0)
    lj = lax.broadcasted_iota(jnp.int32, (LANES, LANES), 1)
    head_mean = jnp.where((li < HEAD_DIM) == (lj < HEAD_DIM), 1.0 / HEAD_DIM, 0.0).astype(_BF16)

    def finish(qi, carry):
        for grp in range(n_g):
            o = acc_ref[qi, grp]
            ms = _dot((o * o).astype(_BF16), head_mean)
            o_ref[grp, _rows(qi), :] = (o * lax.rsqrt(ms + EPS) * g_ref[...]).astype(o_ref.dtype)
        return carry

    lax.fori_loop(0, n_blk, finish, 0)


def _sb_attn(qkv_g, batch, gain2):
    n_tok = qkv_g.shape[1]
    s = n_tok // batch
    n_blk = s // ATTN_TILE
    steps_q, steps_k = _offdiag_steps(n_blk)
    smem = pl.BlockSpec(memory_space=pltpu.SMEM)
    handoff = lambda dtype: pltpu.VMEM((2, 2 * UNIT_GROUPS, ATTN_TILE, ATTN_TILE), dtype)
    return pl.pallas_call(
        _sb_attn_kernel,
        grid=(batch,),
        in_specs=[
            smem, smem,
            pl.BlockSpec((1, LANES), lambda bi: (0, 0)),
            _group_spec(s, 3), _group_spec(s, 4), _group_spec(s, 5),
        ],
        out_specs=pl.BlockSpec((GROUPS, s, LANES), lambda bi: (0, bi, 0)),
        out_shape=jax.ShapeDtypeStruct((GROUPS, n_tok, LANES), _BF16),
        scratch_shapes=[pltpu.VMEM((n_blk, N_SB_HEADS, ATTN_TILE, LANES), _F32),
                        pltpu.VMEM((n_blk, GROUPS, ATTN_TILE, LANES), _F32),
                        handoff(_F32), handoff(_BF16), handoff(_F32), handoff(_BF16),
                        pltpu.VMEM((ATTN_TILE, ATTN_TILE), _BF16),
                        pltpu.VMEM((ATTN_TILE, ATTN_TILE), _F32),
                        pltpu.VMEM((GROUPS, 2, s, LANES), _BF16),
                        pltpu.VMEM((GROUPS, 2, s, LANES), _BF16)],
        compiler_params=pltpu.CompilerParams(vmem_limit_bytes=VMEM_LIMIT_BYTES),
        name="sb_attn",
    )(steps_q, steps_k, gain2, qkv_g, qkv_g, qkv_g)


def _out_mlp_kernel(x_ref, md_ref, ms_ref, wo_ref, g2_ref, wup_ref, wdn_ref,
                    g3_ref, o_ref, m_ref, acc_ref):
    mixed = jnp.concatenate([md_ref[g] for g in range(md_ref.shape[0])]
                            + [ms_ref[g] for g in range(ms_ref.shape[0])], axis=1)
    h = x_ref[...] + _dot(mixed, wo_ref[...])
    m_ref[...] = _rmsnorm(h, g2_ref[...]).astype(_BF16)
    acc_ref[...] = h

    def ff_chunk(c, carry):
        cols = pl.ds(pl.multiple_of(c * FF_CHUNK, FF_CHUNK), FF_CHUNK)
        u = jnp.square(jnp.maximum(_dot(m_ref[...], wup_ref[:, cols]), 0.0))
        acc_ref[...] += _dot(u.astype(_BF16), wdn_ref[cols, :])
        return carry

    lax.fori_loop(0, D_FF // FF_CHUNK, ff_chunk, 0)
    o_ref[...] = _rmsnorm(acc_ref[...], g3_ref[...])


def _out_mlp(x2d, mix_d, mix_s, w_out, g2, w_up, w_dn, g3):
    n_tok = x2d.shape[0]
    const = lambda shape: pl.BlockSpec(shape, lambda i: (0, 0), pipeline_mode=pl.Buffered(1))
    tile = MLP_TOKEN_TILE
    mix = pl.BlockSpec((GROUPS, tile, LANES), lambda i: (0, i, 0))
    return pl.pallas_call(
        _out_mlp_kernel,
        grid=(n_tok // tile,),
        in_specs=[
            pl.BlockSpec((tile, D_MODEL), lambda i: (i, 0)),
            mix, mix,
            const((MIX_WIDTH, D_MODEL)),
            const((1, D_MODEL)),
            const((D_MODEL, D_FF)),
            const((D_FF, D_MODEL)),
            const((1, D_MODEL)),
        ],
        out_specs=pl.BlockSpec((tile, D_MODEL), lambda i: (i, 0)),
        out_shape=jax.ShapeDtypeStruct((n_tok, D_MODEL), _F32),
        scratch_shapes=[pltpu.VMEM((tile, D_MODEL), _BF16),
                        pltpu.VMEM((tile, D_MODEL), _F32)],
        compiler_params=pltpu.CompilerParams(vmem_limit_bytes=VMEM_LIMIT_BYTES),
        name="out_mlp",
    )(x2d, mix_d, mix_s, w_out, g2, w_up, w_dn, g3)


def _query_col_scale():
    qs = ATTN_SCALE * LOG2E
    one = jnp.ones((DIFF_WIDTH,), _F32)
    return jnp.concatenate([qs * one, one, one, qs * one, one, one]).reshape(1, 3 * MIX_WIDTH)


def kernel(x, attn_norm, w_in, lambda_q1, lambda_k1, lambda_q2, lambda_k2, diff_subln, sb_subln,
           w_out, mlp_norm, w_up, w_down, final_norm):
    assert attn_norm.shape[0] == 1, "single-layer block"
    b, s, d = x.shape
    x2d = x.reshape(b * s, d)

    qkv_g = _in_proj(x2d, attn_norm, w_in[0].astype(_BF16), _query_col_scale())

    slopes2 = jnp.asarray([LOG2E * 2.0 ** (-8.0 * (h + 1) / N_DIFF_HEADS)
                           for h in range(N_DIFF_HEADS)], dtype=_F32)
    mix_d = _diff_attn(qkv_g, b, slopes2, lambda_q1, lambda_k1, lambda_q2, lambda_k2, diff_subln)
    mix_s = _sb_attn(qkv_g, b, jnp.tile(sb_subln, (1, 2)))

    out = _out_mlp(x2d, mix_d, mix_s, w_out[0].astype(_BF16), mlp_norm,
                   w_up[0].astype(_BF16), w_down[0].astype(_BF16), final_norm.reshape(1, d))
    return out.reshape(b, s, d)
```
